```python
import math
import jax, jax.numpy as jnp
from jax import lax
import numpy as np

D_MODEL = 2048
BATCH = 8
SEQ = 2048
DEPTH = 1
DEC_BATCH = 32
DEC_SEQ = 4
PAST_LEN = 16384
PAGE_SIZE = 128

N_HEADS_A = 8
HEAD_DIM_A = 64
WIDTH_A = N_HEADS_A * 2 * HEAD_DIM_A
Q_BLOCK = 128
N_HEADS_B = 8
DK_B = 128
DV_B = 128
WIDTH_QK_B = N_HEADS_B * DK_B
WIDTH_V_B = N_HEADS_B * DV_B
CONV_CH_B = 2 * WIDTH_QK_B + WIDTH_V_B
GDN_CONV = 4
GDN_CHUNK = 64
D_FF = 5632
FFN_CONV = 3
EPS = 1e-6
IN_SPLITS = (WIDTH_A, WIDTH_A, WIDTH_A, CONV_CH_B, WIDTH_V_B, N_HEADS_B, N_HEADS_B, D_MODEL, D_MODEL)
IN_COLS = sum(IN_SPLITS)

kernel_name = 'hybrid_diffattn_gdn_convffn_step'


def rms_norm(x, w):
    xf = x.astype(jnp.float32)
    y = xf * lax.rsqrt(jnp.mean(xf * xf, axis=-1, keepdims=True) + EPS)
    return (y * w.astype(jnp.float32)).astype(x.dtype)


def l2_normalize(x):
    return x * lax.rsqrt(jnp.sum(x * x, axis=-1, keepdims=True) + EPS)


def causal_dwconv(x, buf, w):
    width, t = w.shape[0], x.shape[1]
    xp = jnp.concatenate([buf.astype(x.dtype), x], axis=1)
    y = xp[:, 0:t] * w[0]
    for j in range(1, width):
        y = y + xp[:, j:j + t] * w[j]
    return y, xp[:, t:]


def alibi_slopes():
    return 2.0 ** (-8.0 * jnp.arange(1, N_HEADS_A + 1, dtype=jnp.float32) / N_HEADS_A)


def diff_attention_block(q, k, v, q_pos, k_pos, lam, slopes):
    s = jnp.einsum('bthcd,bshcd->bhcts', q, k, preferred_element_type=jnp.float32) * (HEAD_DIM_A ** -0.5)
    dist = (q_pos[:, None] - k_pos[None, :]).astype(jnp.float32)
    s = s - slopes[None, :, None, None, None] * dist
    s = jnp.where(dist >= 0, s, -jnp.inf)
    p = jax.nn.softmax(s, axis=-1)
    a = p[:, :, 0] - lam * p[:, :, 1]
    return jnp.einsum('bhts,bshe->bthe', a.astype(v.dtype), v)


def diff_attention_prompt(q, k, v, lam, slopes):
    b, t = q.shape[0], q.shape[1]
    qb = min(Q_BLOCK, t)
    nb = t // qb
    pos = jnp.arange(t, dtype=jnp.int32)
    q_blocks = jnp.moveaxis(q.reshape(b, nb, qb, N_HEADS_A, 2, HEAD_DIM_A), 1, 0)
    pos_blocks = pos.reshape(nb, qb)
    out = lax.map(lambda a: diff_attention_block(a[0], k, v, a[1], pos, lam, slopes), (q_blocks, pos_blocks))
    return jnp.moveaxis(out, 0, 1).reshape(b, t, N_HEADS_A, 2 * HEAD_DIM_A)


def diff_attention_sample(q, k_new, v_new, cache_k, cache_v, page_table, lam, slopes):
    t = q.shape[1]
    past_len = page_table.shape[1] * cache_k.shape[1]
    q_pos = past_len + jnp.arange(t, dtype=jnp.int32)
    k_pos = jnp.arange(past_len + t, dtype=jnp.int32)

    def one_sequence(args):
        q_i, k_i, v_i, pt = args
        k_past = cache_k[pt].reshape(past_len, N_HEADS_A, 2, HEAD_DIM_A).astype(k_i.dtype)
        v_past = cache_v[pt].reshape(past_len, N_HEADS_A, 2 * HEAD_DIM_A).astype(v_i.dtype)
        k_all = jnp.concatenate([k_past, k_i], axis=0)
        v_all = jnp.concatenate([v_past, v_i], axis=0)
        return diff_attention_block(q_i[None], k_all[None], v_all[None], q_pos, k_pos, lam, slopes)[0]

    return lax.map(one_sequence, (q, k_new, v_new, page_table))


def gated_delta_rule(q, k, v, beta, g, s0):
    b, t, h, _ = q.shape
    dv = v.shape[-1]
    c = min(GDN_CHUNK, t)
    n = -(-t // c)
    pad = n * c - t

    def chunks(a):
        a = jnp.pad(a, [(0, 0), (0, pad)] + [(0, 0)] * (a.ndim - 2))
        a = a.reshape((b, n, c) + a.shape[2:])
        return jnp.moveaxis(a, 3, 1)

    q, k, v, beta, g = chunks(q), chunks(k), chunks(v), chunks(beta), chunks(g)
    g_cum = jnp.cumsum(g, axis=-1)
    idx = jnp.arange(c)
    strict = idx[:, None] > idx[None, :]
    incl = idx[:, None] >= idx[None, :]
    diff = g_cum[..., :, None] - g_cum[..., None, :]
    kb = k * beta[..., None]
    vb = v * beta[..., None]
    lower = jnp.einsum('bhnid,bhnjd->bhnij', kb, k) * jnp.exp(jnp.where(strict, diff, -jnp.inf))
    eye = jnp.eye(c, dtype=jnp.float32)
    tinv = lax.linalg.triangular_solve(lower + eye, jnp.broadcast_to(eye, lower.shape),
                                       left_side=True, lower=True, unit_diagonal=True)
    u = tinv @ vb
    w = tinv @ (kb * jnp.exp(g_cum)[..., None])
    qk = jnp.einsum('bhnid,bhnjd->bhnij', q, k) * jnp.exp(jnp.where(incl, diff, -jnp.inf))
    q_dec = q * jnp.exp(g_cum)[..., None]
    k_dec = k * jnp.exp(g_cum[..., -1:] - g_cum)[..., None]
    g_last = g_cum[..., -1]

    def step(s, xs):
        q_i, k_i, u_i, w_i, qk_i, gl_i = xs
        v_new = u_i - w_i @ s
        o = q_i @ s + qk_i @ v_new
        s = s * jnp.exp(gl_i)[..., None, None] + jnp.einsum('bhck,bhcv->bhkv', k_i, v_new)
        return s, o

    xs = (jnp.moveaxis(q_dec, 2, 0), jnp.moveaxis(k_dec, 2, 0), jnp.moveaxis(u, 2, 0),
          jnp.moveaxis(w, 2, 0), jnp.moveaxis(qk, 2, 0), jnp.moveaxis(g_last, 2, 0))
    s_final, o = lax.scan(step, s0, xs)
    o = jnp.moveaxis(jnp.moveaxis(o, 0, 2), 1, 3).reshape(b, n * c, h, dv)[:, :t]
    return o, s_final


def gated_deltanet(qkv_pre, z, beta_raw, alpha_raw, conv_buf, s0, w_conv, a_log, dt_bias, w_norm):
    b, t, _ = qkv_pre.shape
    qkv, conv_new = causal_dwconv(qkv_pre, conv_buf, w_conv)
    qkv = jax.nn.silu(qkv.astype(jnp.float32))
    q, k, v = jnp.split(qkv, [WIDTH_QK_B, 2 * WIDTH_QK_B], axis=-1)
    q = l2_normalize(q.reshape(b, t, N_HEADS_B, DK_B)) * (DK_B ** -0.5)
    k = l2_normalize(k.reshape(b, t, N_HEADS_B, DK_B))
    v = v.reshape(b, t, N_HEADS_B, DV_B)
    beta = jax.nn.sigmoid(beta_raw.astype(jnp.float32))
    g = -jnp.exp(a_log.astype(jnp.float32)) * jax.nn.softplus(alpha_raw.astype(jnp.float32) + dt_bias.astype(jnp.float32))
    o, s_new = gated_delta_rule(q, k, v, beta, g, s0.astype(jnp.float32))
    o = rms_norm(o, w_norm) * jax.nn.silu(z.astype(jnp.float32).reshape(b, t, N_HEADS_B, DV_B))
    return o.reshape(b, t, WIDTH_V_B).astype(qkv_pre.dtype), conv_new, s_new.astype(s0.dtype)


def conv_ffn(xn, buf, w_in, w_conv, b_conv, w_down):
    gate, up = jnp.split(jnp.einsum('btd,df->btf', xn, w_in), 2, axis=-1)
    gate, buf_new = causal_dwconv(gate, buf, w_conv)
    hidden = jax.nn.silu(gate + b_conv) * up
    return jnp.einsum('btf,fd->btd', hidden, w_down), buf_new


def trunk_layer(x, attend, conv_buf, ssm0, ffn_buf, lam_init,
                w_norm_mix, w_in, w_subln, w_conv_gdn, a_log, dt_bias, w_norm_gdn,
                w_branch_a, w_branch_b, w_out, w_norm_ffn, w_ffn_in, w_ffn_conv, b_ffn_conv, w_ffn_down):
    b, t, _ = x.shape
    xn = rms_norm(x, w_norm_mix)
    proj = jnp.einsum('btd,de->bte', xn, w_in)
    points = np.cumsum(IN_SPLITS)[:-1].tolist()
    q_a, k_a, v_a, qkv_b, z_b, beta_b, alpha_b, gate_a, gate_b = jnp.split(proj, points, axis=-1)
    q_a = q_a.reshape(b, t, N_HEADS_A, 2, HEAD_DIM_A)
    k_a = k_a.reshape(b, t, N_HEADS_A, 2, HEAD_DIM_A)
    v_a = v_a.reshape(b, t, N_HEADS_A, 2 * HEAD_DIM_A)
    o_a = attend(q_a, k_a, v_a)
    o_a = (rms_norm(o_a, w_subln) * (1.0 - lam_init)).reshape(b, t, WIDTH_A)
    o_b, conv_new, ssm_new = gated_deltanet(qkv_b, z_b, beta_b, alpha_b, conv_buf, ssm0,
                                            w_conv_gdn, a_log, dt_bias, w_norm_gdn)
    merged = (jax.nn.sigmoid(gate_a) * (o_a @ w_branch_a)
              + jax.nn.sigmoid(gate_b) * (o_b @ w_branch_b))
    h = x + merged @ w_out
    f, ffn_new = conv_ffn(rms_norm(h, w_norm_ffn), ffn_buf, w_ffn_in, w_ffn_conv, b_ffn_conv, w_ffn_down)
    h = h + f
    k_rows = k_a.reshape(b, t, N_HEADS_A, 2 * HEAD_DIM_A)
    return h, k_rows, v_a, conv_new, ssm_new, ffn_new


def setup_inputs(seed: int = 0) -> dict:
    key = jax.random.key(seed)
    ks = jax.random.split(key, 32)
    f32 = jnp.float32
    n_pages = PAST_LEN // PAGE_SIZE
    n_used = DEC_BATCH * n_pages
    n_pool = n_used + n_used // 4

    def nrm(k, shape, scale):
        return jax.random.normal(k, shape, f32) * scale

    dt = jnp.exp(jax.random.uniform(ks[20], (DEPTH, N_HEADS_B), f32, math.log(1e-3), math.log(1e-1)))
    return {
        'x_prompt': nrm(ks[0], (BATCH, SEQ, D_MODEL), 1.0),
        'x_sample': nrm(ks[1], (DEC_BATCH, DEC_SEQ, D_MODEL), 1.0),
        'cache_k': nrm(ks[2], (DEPTH, n_pool, PAGE_SIZE, N_HEADS_A, 2 * HEAD_DIM_A), 1.0),
        'cache_v': nrm(ks[3], (DEPTH, n_pool, PAGE_SIZE, N_HEADS_A, 2 * HEAD_DIM_A), 1.0),
        'state_conv': nrm(ks[4], (DEPTH, DEC_BATCH, GDN_CONV - 1, CONV_CH_B), 1.0),
        'state_ssm': nrm(ks[5], (DEPTH, DEC_BATCH, N_HEADS_B, DK_B, DV_B), 0.1),
        'state_ffn_conv': nrm(ks[6], (DEPTH, DEC_BATCH, FFN_CONV - 1, D_FF), 1.0),
        'page_table': jax.random.permutation(ks[7], n_pool)[:n_used].reshape(DEC_BATCH, n_pages).astype(jnp.int32),
        'w_norm_mix': 1.0 + nrm(ks[8], (DEPTH, D_MODEL), 0.02),
        'w_in': nrm(ks[9], (DEPTH, D_MODEL, IN_COLS), D_MODEL ** -0.5),
        'lambda_q1': nrm(ks[10], (DEPTH, HEAD_DIM_A), 0.1),
        'lambda_k1': nrm(ks[11], (DEPTH, HEAD_DIM_A), 0.1),
        'lambda_q2': nrm(ks[12], (DEPTH, HEAD_DIM_A), 0.1),
        'lambda_k2': nrm(ks[13], (DEPTH, HEAD_DIM_A), 0.1),
        'w_subln': 1.0 + nrm(ks[14], (DEPTH, 2 * HEAD_DIM_A), 0.02),
        'w_conv_gdn': nrm(ks[15], (DEPTH, GDN_CONV, CONV_CH_B), GDN_CONV ** -0.5),
        'a_log': jnp.log(jax.random.uniform(ks[16], (DEPTH, N_HEADS_B), f32, 1.0, 16.0)),
        'dt_bias': dt + jnp.log(-jnp.expm1(-dt)),
        'w_norm_gdn': 1.0 + nrm(ks[17], (DEPTH, DV_B), 0.02),
        'w_branch_a': nrm(ks[18], (DEPTH, WIDTH_A, D_MODEL), WIDTH_A ** -0.5),
        'w_branch_b': nrm(ks[19], (DEPTH, WIDTH_V_B, D_MODEL), WIDTH_V_B ** -0.5),
        'w_out': nrm(ks[21], (DEPTH, D_MODEL, D_MODEL), D_MODEL ** -0.5),
        'w_norm_ffn': 1.0 + nrm(ks[22], (DEPTH, D_MODEL), 0.02),
        'w_ffn_in': nrm(ks[23], (DEPTH, D_MODEL, 2 * D_FF), D_MODEL ** -0.5),
        'w_ffn_conv': nrm(ks[24], (DEPTH, FFN_CONV, D_FF), FFN_CONV ** -0.5),
        'b_ffn_conv': nrm(ks[25], (DEPTH, D_FF), 0.02),
        'w_ffn_down': nrm(ks[26], (DEPTH, D_FF, D_MODEL), D_FF ** -0.5),
        'w_norm_final': 1.0 + nrm(ks[27], (D_MODEL,), 0.02),
    }


def reference(x_prompt, x_sample, cache_k, cache_v, state_conv, state_ssm, state_ffn_conv, page_table,
              w_norm_mix, w_in, lambda_q1, lambda_k1, lambda_q2, lambda_k2, w_subln, w_conv_gdn, a_log,
              dt_bias, w_norm_gdn, w_branch_a, w_branch_b, w_out, w_norm_ffn, w_ffn_in, w_ffn_conv,
              b_ffn_conv, w_ffn_down, w_norm_final):
    slopes = alibi_slopes()
    bp = x_prompt.shape[0]
    h_p, h_s = x_prompt, x_sample
    kp_l, vp_l, cp_l, sp_l, fp_l = [], [], [], [], []
    ks_l, vs_l, cs_l, ss_l, fs_l = [], [], [], [], []
    for l in range(DEPTH):
        lam_init = 0.8 - 0.6 * math.exp(-0.3 * l)
        lam = (jnp.exp(jnp.sum(lambda_q1[l].astype(jnp.float32) * lambda_k1[l].astype(jnp.float32)))
               - jnp.exp(jnp.sum(lambda_q2[l].astype(jnp.float32) * lambda_k2[l].astype(jnp.float32)))
               + lam_init)
        weights = (w_norm_mix[l], w_in[l], w_subln[l], w_conv_gdn[l], a_log[l], dt_bias[l], w_norm_gdn[l],
                   w_branch_a[l], w_branch_b[l], w_out[l], w_norm_ffn[l], w_ffn_in[l], w_ffn_conv[l],
                   b_ffn_conv[l], w_ffn_down[l])
        ck, cv = cache_k[l], cache_v[l]
        attend_prompt = lambda q, k, v: diff_attention_prompt(q, k, v, lam, slopes)
        attend_sample = lambda q, k, v: diff_attention_sample(q, k, v, ck, cv, page_table, lam, slopes)
        zero_conv = jnp.zeros((bp, GDN_CONV - 1, CONV_CH_B), x_prompt.dtype)
        zero_ssm = jnp.zeros((bp, N_HEADS_B, DK_B, DV_B), x_prompt.dtype)
        zero_ffn = jnp.zeros((bp, FFN_CONV - 1, D_FF), x_prompt.dtype)
        h_p, kp, vp, cp, sp, fp = trunk_layer(h_p, attend_prompt, zero_conv, zero_ssm, zero_ffn, lam_init, *weights)
        h_s, ks, vs, cs, ss, fs = trunk_layer(h_s, attend_sample, state_conv[l], state_ssm[l], state_ffn_conv[l],
                                              lam_init, *weights)
        kp_l.append(kp); vp_l.append(vp); cp_l.append(cp); sp_l.append(sp); fp_l.append(fp)
        ks_l.append(ks); vs_l.append(vs); cs_l.append(cs); ss_l.append(ss); fs_l.append(fs)
    y_prompt = rms_norm(h_p, w_norm_final)
    y_sample = rms_norm(h_s, w_norm_final)
    return (y_prompt, y_sample,
            jnp.stack(kp_l), jnp.stack(vp_l), jnp.stack(cp_l), jnp.stack(sp_l), jnp.stack(fp_l),
            jnp.stack(ks_l), jnp.stack(vs_l), jnp.stack(cs_l), jnp.stack(ss_l), jnp.stack(fs_l))
```

```python
import functools
import math

import jax
import jax.numpy as jnp
from jax import lax
from jax.experimental import pallas as pl
from jax.experimental.pallas import tpu as pltpu

F32 = jnp.float32
BF16 = jnp.bfloat16
EPS = 1e-6
NEG = -1e30
ROW_TILE = 8
SAMPLE_ROWS = 16
LANE = 128
VMEM_LIMIT = 56 * 1024 * 1024

HEAD_DIM_A = 64
GDN_CONV = 4
GDN_CHUNK = 64
FFN_CONV = 3


def _cparams(*sem):
    return pltpu.CompilerParams(dimension_semantics=sem, vmem_limit_bytes=VMEM_LIMIT)


def _dot(a, b):
    return jnp.dot(a, b, preferred_element_type=F32)


def _dot_nt(a, b):
    return lax.dot_general(a, b, (((1,), (1,)), ((), ())), preferred_element_type=F32)


def _dot_tn(a, b):
    return lax.dot_general(a, b, (((0,), (0,)), ((), ())), preferred_element_type=F32)


def _sigmoid(x):
    return 1.0 / (1.0 + jnp.exp(-x))


def _silu(x):
    return x * _sigmoid(x)


def _norm_matmul_kernel(*refs, has_small, norm_rows):
    if has_small:
        x_ref, g_ref, w_ref, w2_ref, o_ref, o2_ref, xn_ref = refs
    else:
        x_ref, g_ref, w_ref, o_ref, xn_ref = refs

    @pl.when(pl.program_id(1) == 0)
    def _():
        def body(r, carry):
            rows = pl.ds(pl.multiple_of(r * norm_rows, norm_rows), norm_rows)
            x = x_ref[rows, :]
            xn = x * lax.rsqrt(jnp.mean(x * x, axis=-1, keepdims=True) + EPS) * g_ref[...]
            xn_ref[rows, :] = xn.astype(BF16)
            return carry
        lax.fori_loop(0, x_ref.shape[0] // norm_rows, body, 0)
        if has_small:
            o2_ref[...] = _dot(xn_ref[...], w2_ref[...])

    o_ref[...] = _dot(xn_ref[...], w_ref[...])


def _norm_matmul(x, g, w, w2=None, *, tm, tn):
    m, k = x.shape
    n = w.shape[1]
    has_small = w2 is not None
    in_specs = [pl.BlockSpec((tm, k), lambda i, j: (i, 0)),
                pl.BlockSpec((1, k), lambda i, j: (0, 0)),
                pl.BlockSpec((k, tn), lambda i, j: (0, j))]
    out_specs = [pl.BlockSpec((tm, tn), lambda i, j: (i, j))]
    out_shape = [jax.ShapeDtypeStruct((m, n), F32)]
    args = [x, g.reshape(1, k), w]
    if has_small:
        in_specs.append(pl.BlockSpec((k, LANE), lambda i, j: (0, 0)))
        out_specs.append(pl.BlockSpec((tm, LANE), lambda i, j: (i, 0)))
        out_shape.append(jax.ShapeDtypeStruct((m, LANE), F32))
        args.append(w2)
    outs = pl.pallas_call(
        functools.partial(_norm_matmul_kernel, has_small=has_small, norm_rows=min(tm, 64)),
        grid=(m // tm, n // tn),
        in_specs=in_specs, out_specs=out_specs, out_shape=out_shape,
        scratch_shapes=[pltpu.VMEM((tm, k), BF16)],
        compiler_params=_cparams("parallel", "arbitrary"),
        name="norm_matmul",
    )(*args)
    return outs if has_small else outs[0]


def _lambda_value(lam_ref, lam_init):
    p = lam_ref[...]
    t1 = jnp.sum(p[0:1, :] * p[1:2, :], axis=-1, keepdims=True)
    t2 = jnp.sum(p[2:3, :] * p[3:4, :], axis=-1, keepdims=True)
    return jnp.exp(t1) - jnp.exp(t2) + lam_init


def _subln(o, w, lam_init):
    n = o * lax.rsqrt(jnp.mean(o * o, axis=-1, keepdims=True) + EPS) * w
    return n * (1.0 - lam_init)


def _attn_prompt_kernel(slope_ref, q_ref, k_ref, v_ref, rel_ref, lam_ref, wsub_ref, o_ref,
                        m_sc, l_sc, acc_sc, *, tq, lam_init):
    h = pl.program_id(1)
    qi = pl.program_id(2)
    ki = pl.program_id(3)

    @pl.when(ki == 0)
    def _():
        m_sc[...] = jnp.full(m_sc.shape, NEG, F32)
        l_sc[...] = jnp.zeros(l_sc.shape, F32)
        acc_sc[...] = jnp.zeros(acc_sc.shape, F32)

    @pl.when(ki <= qi)
    def _():
        q = q_ref[...] * (HEAD_DIM_A ** -0.5)
        lane = lax.broadcasted_iota(jnp.int32, (1, LANE), 1)
        halves = (jnp.where(lane < HEAD_DIM_A, q, 0.0).astype(BF16),
                  jnp.where(lane >= HEAD_DIM_A, q, 0.0).astype(BF16))
        k = k_ref[...].astype(BF16)
        v = v_ref[...].astype(BF16)
        rel = rel_ref[...] + ((ki - qi) * tq).astype(F32)
        bias = slope_ref[h] * rel
        valid = rel <= 0.0
        for c in range(2):
            s = _dot_nt(halves[c], k) + bias
            s = jnp.where(valid, s, NEG)
            m_prev = m_sc[c]
            m_new = jnp.maximum(m_prev, jnp.max(s, axis=-1, keepdims=True))
            alpha = jnp.exp(m_prev - m_new)
            p = jnp.exp(s - m_new)
            l_sc[c] = alpha * l_sc[c] + jnp.sum(p, axis=-1, keepdims=True)
            acc_sc[c] = alpha * acc_sc[c] + _dot(p.astype(BF16), v)
            m_sc[c] = m_new

    @pl.when(ki == qi)
    def _():
        lam = _lambda_value(lam_ref, lam_init)
        o = acc_sc[0] / l_sc[0] - lam * (acc_sc[1] / l_sc[1])
        o_ref[...] = _subln(o, wsub_ref[...], lam_init).astype(o_ref.dtype)


def _attn_prompt(proj, slopes, lam_params, w_subln, *, n_heads, lam_init, tq):
    b, t, _ = proj.shape
    nq = t // tq
    rel = (jnp.arange(tq, dtype=jnp.int32)[None, :] - jnp.arange(tq, dtype=jnp.int32)[:, None]).astype(F32)
    hh = n_heads
    grid_spec = pltpu.PrefetchScalarGridSpec(
        num_scalar_prefetch=0,
        grid=(b, hh, nq, nq),
        in_specs=[
            pl.BlockSpec(memory_space=pltpu.SMEM),
            pl.BlockSpec((None, tq, LANE), lambda bi, h, qi, ki: (bi, qi, h)),
            pl.BlockSpec((None, tq, LANE), lambda bi, h, qi, ki: (bi, jnp.minimum(ki, qi), hh + h)),
            pl.BlockSpec((None, tq, LANE), lambda bi, h, qi, ki: (bi, jnp.minimum(ki, qi), 2 * hh + h)),
            pl.BlockSpec((tq, tq), lambda bi, h, qi, ki: (0, 0)),
            pl.BlockSpec((ROW_TILE, LANE), lambda bi, h, qi, ki: (0, 0)),
            pl.BlockSpec((1, LANE), lambda bi, h, qi, ki: (0, 0)),
        ],
        out_specs=pl.BlockSpec((None, tq, LANE), lambda bi, h, qi, ki: (bi, qi, h)),
        scratch_shapes=[pltpu.VMEM((2, tq, 1), F32), pltpu.VMEM((2, tq, 1), F32),
                        pltpu.VMEM((2, tq, LANE), F32)],
    )
    return pl.pallas_call(
        functools.partial(_attn_prompt_kernel, tq=tq, lam_init=lam_init),
        grid_spec=grid_spec,
        out_shape=jax.ShapeDtypeStruct((b, t, n_heads * LANE), BF16),
        compiler_params=_cparams("parallel", "parallel", "parallel", "arbitrary"),
        name="attn_prompt",
    )(slopes, proj, proj, proj, rel, lam_params, w_subln.reshape(1, LANE))


def _attn_sample_kernel(pt_ref, q_ref, kn_ref, vn_ref, slope_ref, trow_ref, lam_ref, wsub_ref, *rest,
                        n_group, n_heads, dec_seq, past_len, page, lam_init):
    k_refs = rest[:n_group]
    v_refs = rest[n_group:2 * n_group]
    o_ref, qm_sc, m_sc, l_sc, acc_sc = rest[2 * n_group:]
    p_id = pl.program_id(1)
    hs = n_heads * ROW_TILE
    hshift = int(math.log2(n_heads))
    slope = slope_ref[:, 0:1]
    trow = trow_ref[:, 0:1]

    def col_info(n_cols):
        row = lax.broadcasted_iota(jnp.int32, (hs, n_cols), 0)
        col = lax.broadcasted_iota(jnp.int32, (hs, n_cols), 1)
        same_head = (col & (n_heads - 1)) == (row >> int(math.log2(ROW_TILE)))
        return same_head, (col >> hshift).astype(F32)

    @pl.when(p_id == 0)
    def _():
        q = q_ref[...] * (HEAD_DIM_A ** -0.5)
        row = lax.broadcasted_iota(jnp.int32, (hs, LANE), 0)
        lane = lax.broadcasted_iota(jnp.int32, (hs, LANE), 1)
        second_map = ((row >> int(math.log2(dec_seq))) & 1) == 1
        qm = jnp.where((lane >= HEAD_DIM_A) == second_map, q, 0.0).astype(BF16)
        qm_sc[...] = qm
        same_head, key = col_info(ROW_TILE * n_heads)
        dist = trow - key
        s = _dot_nt(qm, kn_ref[...].astype(BF16)) - slope * dist
        s = jnp.where(same_head & (dist >= 0.0) & (key < dec_seq), s, NEG)
        m0 = jnp.max(s, axis=-1, keepdims=True)
        p = jnp.exp(s - m0)
        m_sc[...] = m0
        l_sc[...] = jnp.sum(p, axis=-1, keepdims=True)
        acc_sc[...] = _dot(p.astype(BF16), vn_ref[...].astype(BF16))

    qm = qm_sc[...]
    same_head, key = col_info(page * n_heads)
    m_run = m_sc[...]
    l_run = l_sc[...]
    acc = acc_sc[...]
    for g in range(n_group):
        kg = k_refs[g][...].reshape(page * n_heads, LANE).astype(BF16)
        vg = v_refs[g][...].reshape(page * n_heads, LANE).astype(BF16)
        kpos0 = ((p_id * n_group + g) * page).astype(F32)
        dist = ((past_len + trow) - kpos0) - key
        s = jnp.where(same_head, _dot_nt(qm, kg) - slope * dist, NEG)
        m_new = jnp.maximum(m_run, jnp.max(s, axis=-1, keepdims=True))
        alpha = jnp.exp(m_run - m_new)
        p = jnp.exp(s - m_new)
        l_run = alpha * l_run + jnp.sum(p, axis=-1, keepdims=True)
        acc = alpha * acc + _dot(p.astype(BF16), vg)
        m_run = m_new
    m_sc[...] = m_run
    l_sc[...] = l_run
    acc_sc[...] = acc

    @pl.when(p_id == pl.num_programs(1) - 1)
    def _():
        lam = _lambda_value(lam_ref, lam_init)
        n = acc_sc[...] / l_sc[...]
        o = n - lam * pltpu.roll(n, hs - dec_seq, axis=0)
        o = _subln(o, wsub_ref[...], lam_init)
        pad = jnp.zeros((o_ref.shape[0] - ROW_TILE, LANE), F32)
        for hd in range(n_heads):
            rows = jnp.concatenate([o[hd * ROW_TILE:(hd + 1) * ROW_TILE], pad], axis=0)
            o_ref[:, hd * LANE:(hd + 1) * LANE] = rows.astype(o_ref.dtype)


def _attn_sample(proj, cache_k, cache_v, page_table, slopes, lam_params, w_subln, *,
                 n_heads, dec_seq, lam_init, n_group):
    assert 2 * dec_seq == ROW_TILE
    b, t_pad, _ = proj.shape
    n_pages = page_table.shape[1]
    page = cache_k.shape[1]
    width = n_heads * LANE
    hs = n_heads * ROW_TILE
    past_len = n_pages * page
    q4 = proj[:, :dec_seq, :width].reshape(b, dec_seq, n_heads, LANE).transpose(0, 2, 1, 3)
    q_rows = jnp.concatenate([q4, q4], axis=2).reshape(b, hs, LANE)
    kn_rows = proj[:, :ROW_TILE, width:2 * width].reshape(b, ROW_TILE * n_heads, LANE)
    vn_rows = proj[:, :ROW_TILE, 2 * width:3 * width].reshape(b, ROW_TILE * n_heads, LANE)
    rows = jnp.arange(hs)
    slope_rows = jnp.broadcast_to(slopes[rows // ROW_TILE][:, None], (hs, LANE)).astype(F32)
    trow = jnp.broadcast_to((rows % dec_seq).astype(F32)[:, None], (hs, LANE))
    const = lambda bi, p, pt: (0, 0)

    def page_spec(g):
        return pl.BlockSpec((None, page, n_heads, LANE), lambda bi, p, pt: (pt[bi, p * n_group + g], 0, 0, 0))

    grid_spec = pltpu.PrefetchScalarGridSpec(
        num_scalar_prefetch=1,
        grid=(b, n_pages // n_group),
        in_specs=[
            pl.BlockSpec((None, hs, LANE), lambda bi, p, pt: (bi, 0, 0)),
            pl.BlockSpec((None, ROW_TILE * n_heads, LANE), lambda bi, p, pt: (bi, 0, 0)),
            pl.BlockSpec((None, ROW_TILE * n_heads, LANE), lambda bi, p, pt: (bi, 0, 0)),
            pl.BlockSpec((hs, LANE), const),
            pl.BlockSpec((hs, LANE), const),
            pl.BlockSpec((ROW_TILE, LANE), const),
            pl.BlockSpec((1, LANE), const),
        ] + [page_spec(g) for g in range(n_group)] + [page_spec(g) for g in range(n_group)],
        out_specs=pl.BlockSpec((None, t_pad, width), lambda bi, p, pt: (bi, 0, 0)),
        scratch_shapes=[pltpu.VMEM((hs, LANE), BF16), pltpu.VMEM((hs, 1), F32),
                        pltpu.VMEM((hs, 1), F32), pltpu.VMEM((hs, LANE), F32)],
    )
    return pl.pallas_call(
        functools.partial(_attn_sample_kernel, n_group=n_group, n_heads=n_heads, dec_seq=dec_seq,
                          past_len=float(past_len), page=page, lam_init=lam_init),
        grid_spec=grid_spec,
        out_shape=jax.ShapeDtypeStruct((b, t_pad, width), BF16),
        compiler_params=_cparams("parallel", "arbitrary"),
        name="attn_sample",
    )(page_table, q_rows, kn_rows, vn_rows, slope_rows, trow, lam_params, w_subln.reshape(1, LANE),
      *([cache_k] * n_group), *([cache_v] * n_group))


def _split3_dot(a_bf16, x):
    x1 = x.astype(BF16)
    r1 = x - x1.astype(F32)
    x2 = r1.astype(BF16)
    x3 = (r1 - x2.astype(F32)).astype(BF16)
    return _dot(a_bf16, x1) + _dot(a_bf16, x2) + _dot(a_bf16, x3)


def _gdn_kernel(*refs, chunk, t_valid, n_heads, dk, has_state):
    if has_state:
        (x_ref, z_ref, ba_ref, wconv_ref, prm_ref, wnorm_ref, cstate_ref, s0_ref,
         o_ref, sout_ref, xbuf, s_sc) = refs
    else:
        (x_ref, z_ref, ba_ref, wconv_ref, prm_ref, wnorm_ref,
         o_ref, sout_ref, xbuf, s_sc) = refs
    n = pl.program_id(1)
    c = chunk
    wqk = n_heads * dk

    @pl.when(n == 0)
    def _():
        if has_state:
            xbuf[0:ROW_TILE, :] = cstate_ref[...]
            s_sc[...] = s0_ref[...]
        else:
            xbuf[0:ROW_TILE, :] = jnp.zeros((ROW_TILE, xbuf.shape[1]), F32)
            s_sc[...] = jnp.zeros(s_sc.shape, F32)

    xbuf[ROW_TILE:ROW_TILE + c, :] = x_ref[...]
    y = xbuf[ROW_TILE - 3:ROW_TILE - 3 + c, :] * wconv_ref[0:1, :]
    for j in range(1, GDN_CONV):
        y = y + xbuf[ROW_TILE - 3 + j:ROW_TILE - 3 + j + c, :] * wconv_ref[j:j + 1, :]
    if c > ROW_TILE:
        xbuf[0:ROW_TILE, :] = xbuf[c:c + ROW_TILE, :]
    act = _silu(y)

    ba = ba_ref[...]
    beta_all = _sigmoid(ba)
    sp_in = ba + prm_ref[1:2, :]
    softplus = jnp.maximum(sp_in, 0.0) + jnp.log(1.0 + jnp.exp(-jnp.abs(sp_in)))
    g_all = -jnp.exp(prm_ref[0:1, :]) * softplus
    if t_valid < c:
        rv = lax.broadcasted_iota(jnp.int32, (c, 1), 0) < t_valid
        act = jnp.where(rv, act, 0.0)
        beta_all = jnp.where(rv, beta_all, 0.0)
        g_all = jnp.where(rv, g_all, 0.0)

    ri = lax.broadcasted_iota(jnp.int32, (c, c), 0)
    ci = lax.broadcasted_iota(jnp.int32, (c, c), 1)
    strict = ri > ci
    incl = ri >= ci
    eye = (ri == ci).astype(F32)
    tri = incl.astype(BF16)
    gc_all = _split3_dot(tri, g_all)

    n_double = int(math.log2(c)) - 1
    for h in range(n_heads):
        q = act[:, h * dk:(h + 1) * dk]
        k = act[:, wqk + h * dk:wqk + (h + 1) * dk]
        v = act[:, 2 * wqk + h * dk:2 * wqk + (h + 1) * dk]
        q = q * lax.rsqrt(jnp.sum(q * q, axis=-1, keepdims=True) + EPS) * (dk ** -0.5)
        k = k * lax.rsqrt(jnp.sum(k * k, axis=-1, keepdims=True) + EPS)
        beta = beta_all[:, h:h + 1]
        gc = gc_all[:, n_heads + h:n_heads + h + 1]
        gr = jnp.sum(eye * gc, axis=0, keepdims=True)
        diff = gc - gr
        kb = k * beta
        vb = v * beta
        kbf = k.astype(BF16)
        a = _dot_nt(kb.astype(BF16), kbf) * jnp.exp(jnp.where(strict, diff, NEG))
        xp = -a
        tinv = eye + xp
        for _ in range(n_double):
            xpb = xp.astype(BF16)
            xp = _dot(xpb, xpb)
            tinv = tinv + _dot(xp.astype(BF16), tinv.astype(BF16))
        tinvb = tinv.astype(BF16)
        egc = jnp.exp(gc)
        u = _dot(tinvb, vb.astype(BF16))
        w = _dot(tinvb, (kb * egc).astype(BF16))
        qk = _dot_nt(q.astype(BF16), kbf) * jnp.exp(jnp.where(incl, diff, NEG))
        g_last = gc[c - 1:c, :]
        q_dec = (q * egc).astype(BF16)
        k_dec = (k * jnp.exp(g_last - gc)).astype(BF16)
        s = s_sc[h]
        sb = s.astype(BF16)
        v_new = u - _dot(w.astype(BF16), sb)
        v_newb = v_new.astype(BF16)
        o = _dot(q_dec, sb) + _dot(qk.astype(BF16), v_newb)
        s_sc[h] = s * jnp.exp(g_last) + _dot_tn(k_dec, v_newb)
        on = o * lax.rsqrt(jnp.mean(o * o, axis=-1, keepdims=True) + EPS) * wnorm_ref[...]
        zz = z_ref[:, h * dk:(h + 1) * dk]
        o_ref[:, h * dk:(h + 1) * dk] = (on * _silu(zz)).astype(o_ref.dtype)

    @pl.when(n == pl.num_programs(1) - 1)
    def _():
        sout_ref[...] = s_sc[...]


def _gdn(proj, small, w_conv, a_log, dt_bias, w_norm, conv_state8, s0, *, n_heads, dk, chunk, t_valid,
         qkv_block, z_block):
    b, t, _ = proj.shape
    wv = n_heads * dk
    cch = 3 * wv
    has_state = s0 is not None
    wconv8 = jnp.zeros((ROW_TILE, cch), F32).at[:GDN_CONV].set(w_conv)
    prm = jnp.zeros((ROW_TILE, LANE), F32)
    prm = prm.at[0, n_heads:2 * n_heads].set(a_log).at[1, n_heads:2 * n_heads].set(dt_bias)
    const = lambda bi, n: (0, 0)
    in_specs = [
        pl.BlockSpec((None, chunk, cch), lambda bi, n: (bi, n, qkv_block)),
        pl.BlockSpec((None, chunk, wv), lambda bi, n: (bi, n, z_block)),
        pl.BlockSpec((None, chunk, LANE), lambda bi, n: (bi, n, 0)),
        pl.BlockSpec((ROW_TILE, cch), const),
        pl.BlockSpec((ROW_TILE, LANE), const),
        pl.BlockSpec((1, dk), const),
    ]
    args = [proj, proj, small, wconv8, prm, w_norm.reshape(1, dk)]
    if has_state:
        in_specs += [pl.BlockSpec((None, ROW_TILE, cch), lambda bi, n: (bi, 0, 0)),
                     pl.BlockSpec((None, n_heads, dk, dk), lambda bi, n: (bi, 0, 0, 0))]
        args += [conv_state8, s0]
    o, s_out = pl.pallas_call(
        functools.partial(_gdn_kernel, chunk=chunk, t_valid=t_valid, n_heads=n_heads, dk=dk,
                          has_state=has_state),
        grid=(b, t // chunk),
        in_specs=in_specs,
        out_specs=[pl.BlockSpec((None, chunk, wv), lambda bi, n: (bi, n, 0)),
                   pl.BlockSpec((None, n_heads, dk, dk), lambda bi, n: (bi, 0, 0, 0))],
        out_shape=[jax.ShapeDtypeStruct((b, t, wv), BF16),
                   jax.ShapeDtypeStruct((b, n_heads, dk, dk), F32)],
        scratch_shapes=[pltpu.VMEM((ROW_TILE + chunk, cch), F32), pltpu.VMEM((n_heads, dk, dk), F32)],
        compiler_params=_cparams("parallel", "arbitrary"),
        name="gdn",
    )(*args)
    return o, s_out


def _merge_kernel(oa_ref, ob_ref, ga_ref, gb_ref, wa_ref, wb_ref, o_ref):
    pa = _dot(oa_ref[...], wa_ref[...])
    pb = _dot(ob_ref[...], wb_ref[...])
    o_ref[...] = (_sigmoid(ga_ref[...]) * pa + _sigmoid(gb_ref[...]) * pb).astype(o_ref.dtype)


def _merge(o_a, o_b, proj, w_a, w_b, *, gate_a_col, gate_b_col, tm, tn):
    m, ka = o_a.shape
    n = w_a.shape[1]
    return pl.pallas_call(
        _merge_kernel,
        grid=(m // tm, n // tn),
        in_specs=[pl.BlockSpec((tm, ka), lambda i, j: (i, 0)),
                  pl.BlockSpec((tm, o_b.shape[1]), lambda i, j: (i, 0)),
                  pl.BlockSpec((tm, tn), lambda i, j: (i, gate_a_col // tn + j)),
                  pl.BlockSpec((tm, tn), lambda i, j: (i, gate_b_col // tn + j)),
                  pl.BlockSpec((ka, tn), lambda i, j: (0, j)),
                  pl.BlockSpec((w_b.shape[0], tn), lambda i, j: (0, j))],
        out_specs=pl.BlockSpec((tm, tn), lambda i, j: (i, j)),
        out_shape=jax.ShapeDtypeStruct((m, n), BF16),
        compiler_params=_cparams("parallel", "arbitrary"),
        name="merge",
    )(o_a, o_b, proj, proj, w_a, w_b)


def _matmul_res_kernel(a_ref, w_ref, r_ref, o_ref):
    o_ref[...] = r_ref[...] + _dot(a_ref[...], w_ref[...])


def _matmul_res(a, w, resid, *, tm, tn):
    m, k = a.shape
    n = w.shape[1]
    return pl.pallas_call(
        _matmul_res_kernel,
        grid=(m // tm, n // tn),
        in_specs=[pl.BlockSpec((tm, k), lambda i, j: (i, 0)),
                  pl.BlockSpec((k, tn), lambda i, j: (0, j)),
                  pl.BlockSpec((tm, tn), lambda i, j: (i, j))],
        out_specs=pl.BlockSpec((tm, tn), lambda i, j: (i, j)),
        out_shape=jax.ShapeDtypeStruct((m, n), F32),
        compiler_params=_cparams("parallel", "arbitrary"),
        name="matmul_res",
    )(a, w, resid)


def _ffn_act_kernel(gate_ref, prev_ref, up_ref, wconv_ref, b_ref, o_ref, buf, *, from_state):
    tt = gate_ref.shape[0]
    prev = prev_ref[...]
    if not from_state:
        prev = jnp.where(pl.program_id(1) > 0, prev, 0.0)
    buf[0:ROW_TILE, :] = prev
    buf[ROW_TILE:ROW_TILE + tt, :] = gate_ref[...]
    y = buf[ROW_TILE - 2:ROW_TILE - 2 + tt, :] * wconv_ref[0:1, :]
    for j in range(1, FFN_CONV):
        y = y + buf[ROW_TILE - 2 + j:ROW_TILE - 2 + j + tt, :] * wconv_ref[j:j + 1, :]
    o_ref[...] = (_silu(y + b_ref[...]) * up_ref[...]).astype(o_ref.dtype)


def _ffn_act(gu, w_conv, b_conv, state8, *, d_ff, tt, tf):
    b, t, _ = gu.shape
    from_state = state8 is not None
    nf = d_ff // tf
    wconv8 = jnp.zeros((ROW_TILE, d_ff), F32).at[:FFN_CONV].set(w_conv)
    tpb = tt // ROW_TILE
    if from_state:
        prev_arr = state8
        prev_spec = pl.BlockSpec((None, ROW_TILE, tf), lambda bi, ti, j: (bi, 0, j))
    else:
        prev_arr = gu
        prev_spec = pl.BlockSpec((None, ROW_TILE, tf), lambda bi, ti, j: (bi, jnp.maximum(ti * tpb - 1, 0), j))
    return pl.pallas_call(
        functools.partial(_ffn_act_kernel, from_state=from_state),
        grid=(b, t // tt, nf),
        in_specs=[pl.BlockSpec((None, tt, tf), lambda bi, ti, j: (bi, ti, j)),
                  prev_spec,
                  pl.BlockSpec((None, tt, tf), lambda bi, ti, j: (bi, ti, nf + j)),
                  pl.BlockSpec((ROW_TILE, tf), lambda bi, ti, j: (0, j)),
                  pl.BlockSpec((1, tf), lambda bi, ti, j: (0, j))],
        out_specs=pl.BlockSpec((None, tt, tf), lambda bi, ti, j: (bi, ti, j)),
        out_shape=jax.ShapeDtypeStruct((b, t, d_ff), BF16),
        scratch_shapes=[pltpu.VMEM((ROW_TILE + tt, tf), F32)],
        compiler_params=_cparams("parallel", "parallel", "parallel"),
        name="ffn_act",
    )(gu, prev_arr, gu, wconv8, b_conv.reshape(1, d_ff))


def _down_norm_kernel(a_ref, w_ref, r_ref, g_ref, o_ref, acc_ref):
    kk = pl.program_id(1)

    @pl.when(kk == 0)
    def _():
        acc_ref[...] = r_ref[...]

    acc_ref[...] += _dot(a_ref[...], w_ref[...])

    @pl.when(kk == pl.num_programs(1) - 1)
    def _():
        hh = acc_ref[...]
        o_ref[...] = hh * lax.rsqrt(jnp.mean(hh * hh, axis=-1, keepdims=True) + EPS) * g_ref[...]


def _down_norm(a, w, resid, g, *, tm, tk):
    m, k = a.shape
    n = w.shape[1]
    return pl.pallas_call(
        _down_norm_kernel,
        grid=(m // tm, k // tk),
        in_specs=[pl.BlockSpec((tm, tk), lambda i, kk: (i, kk)),
                  pl.BlockSpec((tk, n), lambda i, kk: (kk, 0)),
                  pl.BlockSpec((tm, n), lambda i, kk: (i, 0)),
                  pl.BlockSpec((1, n), lambda i, kk: (0, 0))],
        out_specs=pl.BlockSpec((tm, n), lambda i, kk: (i, 0)),
        out_shape=jax.ShapeDtypeStruct((m, n), F32),
        scratch_shapes=[pltpu.VMEM((tm, n), F32)],
        compiler_params=_cparams("parallel", "arbitrary"),
        name="down_norm",
    )(a, w, resid, g.reshape(1, n))


def _trunk(x, attend, gdn_state, ffn_state8, wts, *, dims, chunk, t_valid, tm):
    b, t, d = x.shape
    m = b * t
    hb, dk, d_ff = dims["hb"], dims["dk"], dims["d_ff"]
    wa_width = dims["width_a"]
    x2 = x.reshape(m, d)
    proj, small = _norm_matmul(x2, wts["w_norm_mix"], wts["w_in_main"], wts["w_in_small"], tm=tm, tn=1024)
    cols = proj.shape[1]
    proj3 = proj.reshape(b, t, cols)
    small3 = small.reshape(b, t, LANE)
    o_a = attend(proj3)
    conv_state8, s0 = gdn_state
    wv = hb * dk
    o_b, ssm_new = _gdn(proj3, small3, wts["w_conv_gdn"], wts["a_log"], wts["dt_bias"], wts["w_norm_gdn"],
                        conv_state8, s0, n_heads=hb, dk=dk, chunk=chunk, t_valid=t_valid,
                        qkv_block=(3 * wa_width) // (3 * wv), z_block=(3 * wa_width + 3 * wv) // wv)
    gate_a_col = 3 * wa_width + 4 * wv
    merged = _merge(o_a.reshape(m, wa_width), o_b.reshape(m, wv), proj, wts["w_branch_a"], wts["w_branch_b"],
                    gate_a_col=gate_a_col, gate_b_col=gate_a_col + d, tm=tm, tn=1024)
    h = _matmul_res(merged, wts["w_out"], x2, tm=tm, tn=1024)
    gu = _norm_matmul(h, wts["w_norm_ffn"], wts["w_ffn_in"], tm=tm, tn=1024)
    gu3 = gu.reshape(b, t, 2 * d_ff)
    hidden = _ffn_act(gu3, wts["w_ffn_conv"], wts["b_ffn_conv"], ffn_state8, d_ff=d_ff,
                      tt=min(t, 256), tf=512)
    y = _down_norm(hidden.reshape(m, d_ff), wts["w_ffn_down"], h, wts["w_norm_final"], tm=tm, tk=d_ff // 4)
    return y.reshape(b, t, d), proj3, gu3, ssm_new


def kernel(x_prompt, x_sample, cache_k, cache_v, state_conv, state_ssm, state_ffn_conv, page_table, w_norm_mix, w_in, lambda_q1, lambda_k1, lambda_q2, lambda_k2, w_subln, w_conv_gdn, a_log, dt_bias, w_norm_gdn, w_branch_a, w_branch_b, w_out, w_norm_ffn, w_ffn_in, w_ffn_conv, b_ffn_conv, w_ffn_down, w_norm_final):
    depth = w_in.shape[0]
    assert depth == 1, "single-layer trunk"
    l = 0
    bp, seq, d = x_prompt.shape
    bs, dec_seq, _ = x_sample.shape
    ha = cache_k.shape[3]
    width_a = ha * cache_k.shape[4]
    hb, dk = state_ssm.shape[2], state_ssm.shape[3]
    wv = hb * dk
    d_ff = w_ffn_conv.shape[2]
    n_pool, page = cache_k.shape[1], cache_k.shape[2]
    dims = dict(hb=hb, dk=dk, d_ff=d_ff, width_a=width_a)
    lam_init = 0.8 - 0.6 * math.exp(-0.3 * l)
    slopes = 2.0 ** (-8.0 * jnp.arange(1, ha + 1, dtype=F32) / ha)

    c_small = 3 * width_a + 4 * wv
    w_in_l = w_in[l]
    wts = dict(
        w_norm_mix=w_norm_mix[l],
        w_in_main=jnp.concatenate([w_in_l[:, :c_small], w_in_l[:, c_small + 2 * hb:]], axis=1).astype(BF16),
        w_in_small=jnp.pad(w_in_l[:, c_small:c_small + 2 * hb], ((0, 0), (0, LANE - 2 * hb))).astype(BF16),
        w_conv_gdn=w_conv_gdn[l], a_log=a_log[l], dt_bias=dt_bias[l], w_norm_gdn=w_norm_gdn[l],
        w_branch_a=w_branch_a[l].astype(BF16), w_branch_b=w_branch_b[l].astype(BF16),
        w_out=w_out[l].astype(BF16), w_norm_ffn=w_norm_ffn[l], w_ffn_in=w_ffn_in[l].astype(BF16),
        w_ffn_conv=w_ffn_conv[l], b_ffn_conv=b_ffn_conv[l], w_ffn_down=w_ffn_down[l].astype(BF16),
        w_norm_final=w_norm_final,
    )
    lam_params = jnp.zeros((ROW_TILE, LANE), F32)
    lam_params = (lam_params.at[0, :HEAD_DIM_A].set(lambda_q1[l]).at[1, :HEAD_DIM_A].set(lambda_k1[l])
                  .at[2, :HEAD_DIM_A].set(lambda_q2[l]).at[3, :HEAD_DIM_A].set(lambda_k2[l]))

    attend_p = functools.partial(_attn_prompt, slopes=slopes, lam_params=lam_params, w_subln=w_subln[l],
                                 n_heads=ha, lam_init=lam_init, tq=512)
    y_p, proj_p, gu_p, ssm_p = _trunk(x_prompt, attend_p, (None, None), None, wts, dims=dims,
                                      chunk=GDN_CHUNK, t_valid=GDN_CHUNK, tm=512)

    assert dec_seq >= GDN_CONV - 1 and dec_seq >= FFN_CONV - 1
    pad_t = SAMPLE_ROWS - dec_seq
    x_s = jnp.pad(x_sample, ((0, 0), (0, pad_t), (0, 0)))
    conv_state8 = jnp.pad(state_conv[l], ((0, 0), (ROW_TILE - (GDN_CONV - 1), 0), (0, 0)))
    ffn_state8 = jnp.pad(state_ffn_conv[l], ((0, 0), (ROW_TILE - (FFN_CONV - 1), 0), (0, 0)))
    ck = cache_k.reshape(depth * n_pool, page, ha, width_a // ha)
    cv = cache_v.reshape(depth * n_pool, page, ha, width_a // ha)
    attend_s = functools.partial(_attn_sample, cache_k=ck, cache_v=cv, page_table=page_table + l * n_pool,
                                 slopes=slopes, lam_params=lam_params, w_subln=w_subln[l], n_heads=ha,
                                 dec_seq=dec_seq, lam_init=lam_init, n_group=8)
    y_s, proj_s, gu_s, ssm_s = _trunk(x_s, attend_s, (conv_state8, state_ssm[l]), ffn_state8, wts, dims=dims,
                                      chunk=SAMPLE_ROWS, t_valid=dec_seq, tm=bs * SAMPLE_ROWS)

    def outputs(y, proj, gu, ssm, t_real):
        bb = y.shape[0]
        k_rows = proj[:, :t_real, width_a:2 * width_a].reshape(1, bb, t_real, ha, width_a // ha)
        v_rows = proj[:, :t_real, 2 * width_a:3 * width_a].reshape(1, bb, t_real, ha, width_a // ha)
        conv_new = proj[:, t_real - (GDN_CONV - 1):t_real, 3 * width_a:3 * width_a + 3 * wv][None]
        ffn_new = gu[:, t_real - (FFN_CONV - 1):t_real, :d_ff][None]
        return y[:, :t_real], k_rows, v_rows, conv_new, ssm[None], ffn_new

    yp, kp, vp, cp, sp, fp = outputs(y_p, proj_p, gu_p, ssm_p, seq)
    ys, ks, vs, cs, ss, fs = outputs(y_s, proj_s, gu_s, ssm_s, dec_seq)
    return (yp, ys, kp, vp, cp, sp, fp, ks, vs, cs, ss, fs)
```

```python
import functools
import math

import jax
import jax.numpy as jnp
from jax import lax
from jax.experimental import pallas as pl
from jax.experimental.pallas import tpu as pltpu

F32 = jnp.float32
BF16 = jnp.bfloat16
EPS = 1e-6
NEG = -1e30
LOG2E = math.log2(math.e)
ROW_TILE = 8
SAMPLE_ROWS = 16
LANE = 128
VMEM_LIMIT = 56 * 1024 * 1024

HEAD_DIM_A = 64
GDN_CONV = 4
GDN_CHUNK = 64
FFN_CONV = 3


def _cparams(*sem):
    return pltpu.CompilerParams(dimension_semantics=sem, vmem_limit_bytes=VMEM_LIMIT)


def _dot(a, b):
    return jnp.dot(a, b, preferred_element_type=F32)


def _dot_nt(a, b):
    return lax.dot_general(a, b, (((1,), (1,)), ((), ())), preferred_element_type=F32)


def _dot_tn(a, b):
    return lax.dot_general(a, b, (((0,), (0,)), ((), ())), preferred_element_type=F32)


def _sigmoid(x):
    return 0.5 * jnp.tanh(0.5 * x) + 0.5


def _silu(x):
    return x * _sigmoid(x)


def _norm_matmul_kernel(*refs, has_small, norm_rows):
    if has_small:
        x_ref, g_ref, w_ref, w2_ref, o_ref, o2_ref, xn_ref = refs
    else:
        x_ref, g_ref, w_ref, o_ref, xn_ref = refs

    @pl.when(pl.program_id(1) == 0)
    def _():
        def body(r, carry):
            rows = pl.ds(pl.multiple_of(r * norm_rows, norm_rows), norm_rows)
            x = x_ref[rows, :]
            xn = x * lax.rsqrt(jnp.mean(x * x, axis=-1, keepdims=True) + EPS) * g_ref[...]
            xn_ref[rows, :] = xn.astype(BF16)
            return carry
        lax.fori_loop(0, x_ref.shape[0] // norm_rows, body, 0)
        if has_small:
            o2_ref[...] = _dot(xn_ref[...], w2_ref[...])

    o_ref[...] = _dot(xn_ref[...], w_ref[...])


def _norm_matmul(x, g, w, w2=None, *, tm, tn):
    m, k = x.shape
    n = w.shape[1]
    has_small = w2 is not None
    in_specs = [pl.BlockSpec((tm, k), lambda i, j: (i, 0)),
                pl.BlockSpec((1, k), lambda i, j: (0, 0)),
                pl.BlockSpec((k, tn), lambda i, j: (0, j))]
    out_specs = [pl.BlockSpec((tm, tn), lambda i, j: (i, j))]
    out_shape = [jax.ShapeDtypeStruct((m, n), F32)]
    args = [x, g.reshape(1, k), w]
    if has_small:
        in_specs.append(pl.BlockSpec((k, LANE), lambda i, j: (0, 0)))
        out_specs.append(pl.BlockSpec((tm, LANE), lambda i, j: (i, 0)))
        out_shape.append(jax.ShapeDtypeStruct((m, LANE), F32))
        args.append(w2)
    outs = pl.pallas_call(
        functools.partial(_norm_matmul_kernel, has_small=has_small, norm_rows=min(tm, 64)),
        grid=(m // tm, n // tn),
        in_specs=in_specs, out_specs=out_specs, out_shape=out_shape,
        scratch_shapes=[pltpu.VMEM((tm, k), BF16)],
        compiler_params=_cparams("parallel", "arbitrary"),
        name="norm_matmul",
    )(*args)
    return outs if has_small else outs[0]


def _lambda_value(lam_ref, lam_init):
    p = lam_ref[...]
    t1 = jnp.sum(p[0:1, :] * p[1:2, :], axis=-1, keepdims=True)
    t2 = jnp.sum(p[2:3, :] * p[3:4, :], axis=-1, keepdims=True)
    return jnp.exp(t1) - jnp.exp(t2) + lam_init


def _subln(o, w, lam_init):
    n = o * lax.rsqrt(jnp.mean(o * o, axis=-1, keepdims=True) + EPS) * w
    return n * (1.0 - lam_init)


def _attn_prompt_kernel(slope_ref, q_ref, k_ref, v_ref, rel_ref, lam_ref, wsub_ref, o_ref,
                        kb_sc, vt_sc, bias_sc, q2_sc, m_sc, l_sc, acc_sc, *, tq, qcols, lam_init):
    h = pl.program_id(1)
    qi = pl.program_id(2)

    slope2 = slope_ref[h] * LOG2E

    @pl.when(qi == 0)
    def _():
        kb_sc[...] = k_ref[...].astype(BF16)
        vt_sc[...] = v_ref[...].T.astype(BF16)
        bias_sc[...] = slope2 * rel_ref[...]

    q = q_ref[...] * (HEAD_DIM_A ** -0.5 * LOG2E)
    lane = lax.broadcasted_iota(jnp.int32, (1, LANE), 1)
    q2_sc[0:tq, :] = jnp.where(lane < HEAD_DIM_A, q, 0.0).astype(BF16)
    q2_sc[tq:2 * tq, :] = jnp.where(lane >= HEAD_DIM_A, q, 0.0).astype(BF16)
    m_sc[...] = jnp.full(m_sc.shape, NEG, F32)
    l_sc[...] = jnp.zeros(l_sc.shape, F32)
    acc_sc[...] = jnp.zeros(acc_sc.shape, F32)

    def key_block(ki, masked):
        k0 = pl.multiple_of(ki * tq, tq)
        kt = kb_sc[pl.ds(k0, tq), :]
        vt = vt_sc[:, pl.ds(k0, tq)]
        shift = slope2 * ((ki - qi) * tq).astype(F32)
        cols = [pl.ds(j * qcols, qcols) for j in range(2 * tq // qcols)]
        scores = [_dot_nt(kt, q2_sc[c, :]) for c in cols]
        for c, s in zip(cols, scores):
            bias = bias_sc[:, c]
            if masked:
                bias = jnp.where(rel_ref[:, c] <= 0.0, bias, NEG)
            s = s + bias
            m_prev = m_sc[:, c]
            m_new = jnp.maximum(m_prev, jnp.max(s, axis=0, keepdims=True) + shift)
            alpha = jnp.exp2(m_prev - m_new)
            p = jnp.exp2(s - (m_new - shift))
            l_sc[:, c] = alpha * l_sc[:, c] + jnp.sum(p, axis=0, keepdims=True)
            acc_sc[:, c] = alpha * acc_sc[:, c] + _dot(vt, p.astype(BF16))
            m_sc[:, c] = m_new

    def body(ki, carry):
        key_block(ki, False)
        return carry

    lax.fori_loop(0, qi, body, 0)
    key_block(qi, True)

    lam = _lambda_value(lam_ref, lam_init)
    n = acc_sc[...] / l_sc[...]
    o = (n[:, 0:tq] - lam * n[:, tq:2 * tq]).T
    o_ref[...] = _subln(o, wsub_ref[...], lam_init).astype(o_ref.dtype)


def _attn_prompt(proj, slopes, lam_params, w_subln, *, n_heads, lam_init, tq):
    b, t, _ = proj.shape
    rel = (jnp.arange(tq, dtype=jnp.int32)[:, None] - jnp.arange(tq, dtype=jnp.int32)[None, :]).astype(F32)
    rel = jnp.concatenate([rel, rel], axis=1)
    hh = n_heads
    grid_spec = pltpu.PrefetchScalarGridSpec(
        num_scalar_prefetch=0,
        grid=(b, hh, t // tq),
        in_specs=[
            pl.BlockSpec(memory_space=pltpu.SMEM),
            pl.BlockSpec((None, tq, LANE), lambda bi, h, qi: (bi, qi, h)),
            pl.BlockSpec((None, t, LANE), lambda bi, h, qi: (bi, 0, hh + h)),
            pl.BlockSpec((None, t, LANE), lambda bi, h, qi: (bi, 0, 2 * hh + h)),
            pl.BlockSpec((tq, 2 * tq), lambda bi, h, qi: (0, 0)),
            pl.BlockSpec((ROW_TILE, LANE), lambda bi, h, qi: (0, 0)),
            pl.BlockSpec((1, LANE), lambda bi, h, qi: (0, 0)),
        ],
        out_specs=pl.BlockSpec((None, tq, LANE), lambda bi, h, qi: (bi, qi, h)),
        scratch_shapes=[pltpu.VMEM((t, LANE), BF16), pltpu.VMEM((LANE, t), BF16),
                        pltpu.VMEM((tq, 2 * tq), F32),
                        pltpu.VMEM((2 * tq, LANE), BF16), pltpu.VMEM((1, 2 * tq), F32),
                        pltpu.VMEM((1, 2 * tq), F32), pltpu.VMEM((LANE, 2 * tq), F32)],
    )
    return pl.pallas_call(
        functools.partial(_attn_prompt_kernel, tq=tq, qcols=2 * tq, lam_init=lam_init),
        grid_spec=grid_spec,
        out_shape=jax.ShapeDtypeStruct((b, t, n_heads * LANE), BF16),
        compiler_params=_cparams("parallel", "parallel", "arbitrary"),
        name="attn_prompt",
    )(slopes, proj, proj, proj, rel, lam_params, w_subln.reshape(1, LANE))


def _attn_sample_kernel(pt_ref, q_ref, kn_ref, vn_ref, slope_ref, trow_ref, lam_ref, wsub_ref, *rest,
                        n_group, n_heads, dec_seq, past_len, page, lam_init):
    k_refs = rest[:n_group]
    v_refs = rest[n_group:2 * n_group]
    o_ref, qm_sc, bias_sc, m_sc, l_sc, acc_sc = rest[2 * n_group:]
    p_id = pl.program_id(1)
    hs = n_heads * ROW_TILE
    hshift = int(math.log2(n_heads))
    slope2 = slope_ref[:, 0:1] * LOG2E
    trow = trow_ref[:, 0:1]

    def col_info(n_cols):
        row = lax.broadcasted_iota(jnp.int32, (hs, n_cols), 0)
        col = lax.broadcasted_iota(jnp.int32, (hs, n_cols), 1)
        same_head = (col & (n_heads - 1)) == (row >> int(math.log2(ROW_TILE)))
        return same_head, (col >> hshift).astype(F32)

    @pl.when(p_id == 0)
    def _():
        q = q_ref[...] * (HEAD_DIM_A ** -0.5 * LOG2E)
        row = lax.broadcasted_iota(jnp.int32, (hs, LANE), 0)
        lane = lax.broadcasted_iota(jnp.int32, (hs, LANE), 1)
        second_map = ((row >> int(math.log2(dec_seq))) & 1) == 1
        qm = jnp.where((lane >= HEAD_DIM_A) == second_map, q, 0.0).astype(BF16)
        qm_sc[...] = qm
        same_head, key = col_info(page * n_heads)
        bias_sc[...] = jnp.where(same_head, slope2 * key, NEG)
        same_head, key = col_info(ROW_TILE * n_heads)
        dist = trow - key
        s = _dot_nt(qm, kn_ref[...].astype(BF16)) - slope2 * dist
        s = jnp.where(same_head & (dist >= 0.0) & (key < dec_seq), s, NEG)
        m0 = jnp.max(s, axis=-1, keepdims=True)
        p = jnp.exp2(s - m0)
        m_sc[...] = m0
        l_sc[...] = jnp.sum(p, axis=-1, keepdims=True)
        acc_sc[...] = _dot(p.astype(BF16), vn_ref[...].astype(BF16))

    qm = qm_sc[...]
    bias = bias_sc[...]
    scores, shifts = [], []
    for g in range(n_group):
        kg = k_refs[g][...].reshape(page * n_heads, LANE).astype(BF16)
        scores.append(_dot_nt(qm, kg) + bias)
        kpos0 = ((p_id * n_group + g) * page).astype(F32)
        shifts.append(slope2 * ((past_len + trow) - kpos0))
    m_prev = m_sc[...]
    m_new = m_prev
    for s, sh in zip(scores, shifts):
        m_new = jnp.maximum(m_new, jnp.max(s, axis=-1, keepdims=True) - sh)
    alpha = jnp.exp2(m_prev - m_new)
    l_new = alpha * l_sc[...]
    acc = alpha * acc_sc[...]
    for g in range(n_group):
        p = jnp.exp2(scores[g] - (m_new + shifts[g]))
        l_new = l_new + jnp.sum(p, axis=-1, keepdims=True)
        vg = v_refs[g][...].reshape(page * n_heads, LANE).astype(BF16)
        acc = acc + _dot(p.astype(BF16), vg)
    m_sc[...] = m_new
    l_sc[...] = l_new
    acc_sc[...] = acc

    @pl.when(p_id == pl.num_programs(1) - 1)
    def _():
        lam = _lambda_value(lam_ref, lam_init)
        n = acc_sc[...] / l_sc[...]
        o = n - lam * pltpu.roll(n, hs - dec_seq, axis=0)
        o = _subln(o, wsub_ref[...], lam_init)
        pad = jnp.zeros((o_ref.shape[0] - ROW_TILE, LANE), F32)
        for hd in range(n_heads):
            rows = jnp.concatenate([o[hd * ROW_TILE:(hd + 1) * ROW_TILE], pad], axis=0)
            o_ref[:, hd * LANE:(hd + 1) * LANE] = rows.astype(o_ref.dtype)


def _attn_sample(proj, cache_k, cache_v, page_table, slopes, lam_params, w_subln, *,
                 n_heads, dec_seq, lam_init, n_group):
    assert 2 * dec_seq == ROW_TILE
    b, t_pad, _ = proj.shape
    n_pages = page_table.shape[1]
    page = cache_k.shape[1]
    width = n_heads * LANE
    hs = n_heads * ROW_TILE
    past_len = n_pages * page
    q4 = proj[:, :dec_seq, :width].reshape(b, dec_seq, n_heads, LANE).transpose(0, 2, 1, 3)
    q_rows = jnp.concatenate([q4, q4], axis=2).reshape(b, hs, LANE)
    kn_rows = proj[:, :ROW_TILE, width:2 * width].reshape(b, ROW_TILE * n_heads, LANE)
    vn_rows = proj[:, :ROW_TILE, 2 * width:3 * width].reshape(b, ROW_TILE * n_heads, LANE)
    rows = jnp.arange(hs)
    slope_rows = jnp.broadcast_to(slopes[rows // ROW_TILE][:, None], (hs, LANE)).astype(F32)
    trow = jnp.broadcast_to((rows % dec_seq).astype(F32)[:, None], (hs, LANE))
    const = lambda bi, p, pt: (0, 0)

    def page_spec(g):
        return pl.BlockSpec((None, page, n_heads, LANE), lambda bi, p, pt: (pt[bi, p * n_group + g], 0, 0, 0))

    grid_spec = pltpu.PrefetchScalarGridSpec(
        num_scalar_prefetch=1,
        grid=(b, n_pages // n_group),
        in_specs=[
            pl.BlockSpec((None, hs, LANE), lambda bi, p, pt: (bi, 0, 0)),
            pl.BlockSpec((None, ROW_TILE * n_heads, LANE), lambda bi, p, pt: (bi, 0, 0)),
            pl.BlockSpec((None, ROW_TILE * n_heads, LANE), lambda bi, p, pt: (bi, 0, 0)),
            pl.BlockSpec((hs, LANE), const),
            pl.BlockSpec((hs, LANE), const),
            pl.BlockSpec((ROW_TILE, LANE), const),
            pl.BlockSpec((1, LANE), const),
        ] + [page_spec(g) for g in range(n_group)] + [page_spec(g) for g in range(n_group)],
        out_specs=pl.BlockSpec((None, t_pad, width), lambda bi, p, pt: (bi, 0, 0)),
        scratch_shapes=[pltpu.VMEM((hs, LANE), BF16), pltpu.VMEM((hs, page * n_heads), F32),
                        pltpu.VMEM((hs, 1), F32), pltpu.VMEM((hs, 1), F32), pltpu.VMEM((hs, LANE), F32)],
    )
    return pl.pallas_call(
        functools.partial(_attn_sample_kernel, n_group=n_group, n_heads=n_heads, dec_seq=dec_seq,
                          past_len=float(past_len), page=page, lam_init=lam_init),
        grid_spec=grid_spec,
        out_shape=jax.ShapeDtypeStruct((b, t_pad, width), BF16),
        compiler_params=_cparams("parallel", "arbitrary"),
        name="attn_sample",
    )(page_table, q_rows, kn_rows, vn_rows, slope_rows, trow, lam_params, w_subln.reshape(1, LANE),
      *([cache_k] * n_group), *([cache_v] * n_group))


def _split3_dot(a_bf16, x):
    x1 = x.astype(BF16)
    r1 = x - x1.astype(F32)
    x2 = r1.astype(BF16)
    x3 = (r1 - x2.astype(F32)).astype(BF16)
    return _dot(a_bf16, x1) + _dot(a_bf16, x2) + _dot(a_bf16, x3)


def _gdn_kernel(*refs, chunk, t_valid, n_heads, dk, has_state):
    if has_state:
        (x_ref, z_ref, ba_ref, wconv_ref, prm_ref, wnorm_ref, cstate_ref, s0_ref,
         o_ref, sout_ref, xbuf, s_sc) = refs
    else:
        (x_ref, z_ref, ba_ref, wconv_ref, prm_ref, wnorm_ref,
         o_ref, sout_ref, xbuf, s_sc) = refs
    n = pl.program_id(1)
    c = chunk
    wqk = n_heads * dk

    @pl.when(n == 0)
    def _():
        if has_state:
            xbuf[0:ROW_TILE, :] = cstate_ref[...]
            s_sc[...] = s0_ref[...]
        else:
            xbuf[0:ROW_TILE, :] = jnp.zeros((ROW_TILE, xbuf.shape[1]), F32)
            s_sc[...] = jnp.zeros(s_sc.shape, F32)

    xbuf[ROW_TILE:ROW_TILE + c, :] = x_ref[...]
    xfull = xbuf[...]
    y = xfull[ROW_TILE:] * wconv_ref[GDN_CONV - 1:GDN_CONV, :]
    for d in range(1, GDN_CONV):
        y = y + pltpu.roll(xfull, d, axis=0)[ROW_TILE:] * wconv_ref[GDN_CONV - 1 - d:GDN_CONV - d, :]
    if c > ROW_TILE:
        xbuf[0:ROW_TILE, :] = xfull[c:c + ROW_TILE]
    act = _silu(y)

    ba = ba_ref[...]
    beta_all = _sigmoid(ba)
    sp_in = ba + prm_ref[1:2, :]
    softplus = jnp.maximum(sp_in, 0.0) + jnp.log(1.0 + jnp.exp(-jnp.abs(sp_in)))
    g_all = -jnp.exp(prm_ref[0:1, :]) * softplus
    if t_valid < c:
        rv = lax.broadcasted_iota(jnp.int32, (c, 1), 0) < t_valid
        act = jnp.where(rv, act, 0.0)
        beta_all = jnp.where(rv, beta_all, 0.0)
        g_all = jnp.where(rv, g_all, 0.0)

    ri = lax.broadcasted_iota(jnp.int32, (c, c), 0)
    ci = lax.broadcasted_iota(jnp.int32, (c, c), 1)
    strict = ri > ci
    incl = ri >= ci
    eye = (ri == ci).astype(F32)
    tri = incl.astype(BF16)
    gc_all = _split3_dot(tri, g_all)

    heads = range(n_heads)
    n_double = int(math.log2(c)) - 1
    qs, ks, vs = [], [], []
    for h in heads:
        q = act[:, h * dk:(h + 1) * dk]
        k = act[:, wqk + h * dk:wqk + (h + 1) * dk]
        qs.append(q * lax.rsqrt(jnp.sum(q * q, axis=-1, keepdims=True) + EPS) * (dk ** -0.5))
        ks.append(k * lax.rsqrt(jnp.sum(k * k, axis=-1, keepdims=True) + EPS))
        vs.append(act[:, 2 * wqk + h * dk:2 * wqk + (h + 1) * dk])
    betas = [beta_all[:, h:h + 1] for h in heads]
    gcs = [gc_all[:, n_heads + h:n_heads + h + 1] for h in heads]
    decays = [jnp.exp(jnp.where(incl, gc - jnp.sum(eye * gc, axis=0, keepdims=True), NEG)) for gc in gcs]
    kbs = [k * b for k, b in zip(ks, betas)]
    kbf = [k.astype(BF16) for k in ks]
    kq = [_dot_nt(jnp.concatenate([kb.astype(BF16), q.astype(BF16)], axis=0), kf)
          for kb, q, kf in zip(kbs, qs, kbf)]
    xps = [jnp.where(strict, -(m[:c] * d), 0.0) for m, d in zip(kq, decays)]
    tinvs = [eye + x for x in xps]
    xps = [_dot(x.astype(BF16), x.astype(BF16)) for x in xps]
    for i in range(n_double):
        xpb = [x.astype(BF16) for x in xps]
        tinvs = [t + _dot(xb, t.astype(BF16)) for t, xb in zip(tinvs, xpb)]
        if i < n_double - 1:
            xps = [_dot(xb, xb) for xb in xpb]
    egcs = [jnp.exp(gc) for gc in gcs]
    uw = [_dot(t.astype(BF16), jnp.concatenate([(v * b).astype(BF16), (kb * e).astype(BF16)], axis=1))
          for t, v, b, kb, e in zip(tinvs, vs, betas, kbs, egcs)]
    g_lasts = [gc[c - 1:c, :] for gc in gcs]
    k_decs = [(k * jnp.exp(gl - gc)).astype(BF16) for k, gl, gc in zip(ks, g_lasts, gcs)]
    states = [s_sc[h] for h in heads]
    ws_qs = [_dot(jnp.concatenate([m[:, dk:].astype(BF16), (q * e).astype(BF16)], axis=0), s.astype(BF16))
             for m, q, e, s in zip(uw, qs, egcs, states)]
    v_newb = [(m[:, :dk] - x[:c]).astype(BF16) for m, x in zip(uw, ws_qs)]
    outs = [x[c:] + _dot((m[c:] * d).astype(BF16), vn) for x, m, d, vn in zip(ws_qs, kq, decays, v_newb)]
    for h in heads:
        s_sc[h] = states[h] * jnp.exp(g_lasts[h]) + _dot_tn(k_decs[h], v_newb[h])
    for h in heads:
        o = outs[h]
        on = o * lax.rsqrt(jnp.mean(o * o, axis=-1, keepdims=True) + EPS) * wnorm_ref[...]
        zz = z_ref[:, h * dk:(h + 1) * dk]
        o_ref[:, h * dk:(h + 1) * dk] = (on * _silu(zz)).astype(o_ref.dtype)

    @pl.when(n == pl.num_programs(1) - 1)
    def _():
        sout_ref[...] = s_sc[...]


def _gdn(proj, small, w_conv, a_log, dt_bias, w_norm, conv_state8, s0, *, n_heads, dk, chunk, t_valid,
         qkv_block, z_block):
    b, t, _ = proj.shape
    wv = n_heads * dk
    cch = 3 * wv
    has_state = s0 is not None
    wconv8 = jnp.zeros((ROW_TILE, cch), F32).at[:GDN_CONV].set(w_conv)
    prm = jnp.zeros((ROW_TILE, LANE), F32)
    prm = prm.at[0, n_heads:2 * n_heads].set(a_log).at[1, n_heads:2 * n_heads].set(dt_bias)
    const = lambda bi, n: (0, 0)
    in_specs = [
        pl.BlockSpec((None, chunk, cch), lambda bi, n: (bi, n, qkv_block)),
        pl.BlockSpec((None, chunk, wv), lambda bi, n: (bi, n, z_block)),
        pl.BlockSpec((None, chunk, LANE), lambda bi, n: (bi, n, 0)),
        pl.BlockSpec((ROW_TILE, cch), const),
        pl.BlockSpec((ROW_TILE, LANE), const),
        pl.BlockSpec((1, dk), const),
    ]
    args = [proj, proj, small, wconv8, prm, w_norm.reshape(1, dk)]
    if has_state:
        in_specs += [pl.BlockSpec((None, ROW_TILE, cch), lambda bi, n: (bi, 0, 0)),
                     pl.BlockSpec((None, n_heads, dk, dk), lambda bi, n: (bi, 0, 0, 0))]
        args += [conv_state8, s0]
    o, s_out = pl.pallas_call(
        functools.partial(_gdn_kernel, chunk=chunk, t_valid=t_valid, n_heads=n_heads, dk=dk,
                          has_state=has_state),
        grid=(b, t // chunk),
        in_specs=in_specs,
        out_specs=[pl.BlockSpec((None, chunk, wv), lambda bi, n: (bi, n, 0)),
                   pl.BlockSpec((None, n_heads, dk, dk), lambda bi, n: (bi, 0, 0, 0))],
        out_shape=[jax.ShapeDtypeStruct((b, t, wv), BF16),
                   jax.ShapeDtypeStruct((b, n_heads, dk, dk), F32)],
        scratch_shapes=[pltpu.VMEM((ROW_TILE + chunk, cch), F32), pltpu.VMEM((n_heads, dk, dk), F32)],
        compiler_params=_cparams("parallel", "arbitrary"),
        name="gdn",
    )(*args)
    return o, s_out


def _merge_kernel(oa_ref, ob_ref, ga_ref, gb_ref, wa_ref, wb_ref, o_ref):
    pa = _dot(oa_ref[...], wa_ref[...])
    pb = _dot(ob_ref[...], wb_ref[...])
    o_ref[...] = (_sigmoid(ga_ref[...]) * pa + _sigmoid(gb_ref[...]) * pb).astype(o_ref.dtype)


def _merge(o_a, o_b, proj, w_a, w_b, *, gate_a_col, gate_b_col, tm, tn):
    m, ka = o_a.shape
    n = w_a.shape[1]
    return pl.pallas_call(
        _merge_kernel,
        grid=(m // tm, n // tn),
        in_specs=[pl.BlockSpec((tm, ka), lambda i, j: (i, 0)),
                  pl.BlockSpec((tm, o_b.shape[1]), lambda i, j: (i, 0)),
                  pl.BlockSpec((tm, tn), lambda i, j: (i, gate_a_col // tn + j)),
                  pl.BlockSpec((tm, tn), lambda i, j: (i, gate_b_col // tn + j)),
                  pl.BlockSpec((ka, tn), lambda i, j: (0, j)),
                  pl.BlockSpec((w_b.shape[0], tn), lambda i, j: (0, j))],
        out_specs=pl.BlockSpec((tm, tn), lambda i, j: (i, j)),
        out_shape=jax.ShapeDtypeStruct((m, n), BF16),
        compiler_params=_cparams("parallel", "arbitrary"),
        name="merge",
    )(o_a, o_b, proj, proj, w_a, w_b)


def _matmul_res_kernel(a_ref, w_ref, r_ref, o_ref):
    o_ref[...] = r_ref[...] + _dot(a_ref[...], w_ref[...])


def _matmul_res(a, w, resid, *, tm, tn):
    m, k = a.shape
    n = w.shape[1]
    return pl.pallas_call(
        _matmul_res_kernel,
        grid=(m // tm, n // tn),
        in_specs=[pl.BlockSpec((tm, k), lambda i, j: (i, 0)),
                  pl.BlockSpec((k, tn), lambda i, j: (0, j)),
                  pl.BlockSpec((tm, tn), lambda i, j: (i, j))],
        out_specs=pl.BlockSpec((tm, tn), lambda i, j: (i, j)),
        out_shape=jax.ShapeDtypeStruct((m, n), F32),
        compiler_params=_cparams("parallel", "arbitrary"),
        name="matmul_res",
    )(a, w, resid)


def _ffn_act_kernel(gate_ref, prev_ref, up_ref, wconv_ref, b_ref, o_ref, buf, *, from_state):
    tt = gate_ref.shape[0]
    prev = prev_ref[...]
    if not from_state:
        prev = jnp.where(pl.program_id(1) > 0, prev, 0.0)
    buf[0:ROW_TILE, :] = prev
    buf[ROW_TILE:ROW_TILE + tt, :] = gate_ref[...]
    full = buf[...]
    y = full[ROW_TILE:] * wconv_ref[FFN_CONV - 1:FFN_CONV, :] + b_ref[...]
    for d in range(1, FFN_CONV):
        y = y + pltpu.roll(full, d, axis=0)[ROW_TILE:] * wconv_ref[FFN_CONV - 1 - d:FFN_CONV - d, :]
    o_ref[...] = (_silu(y) * up_ref[...]).astype(o_ref.dtype)


def _ffn_act(gu, w_conv, b_conv, state8, *, d_ff, tt, tf):
    b, t, _ = gu.shape
    from_state = state8 is not None
    nf = d_ff // tf
    wconv8 = jnp.zeros((ROW_TILE, d_ff), F32).at[:FFN_CONV].set(w_conv)
    tpb = tt // ROW_TILE
    if from_state:
        prev_arr = state8
        prev_spec = pl.BlockSpec((None, ROW_TILE, tf), lambda bi, ti, j: (bi, 0, j))
    else:
        prev_arr = gu
        prev_spec = pl.BlockSpec((None, ROW_TILE, tf), lambda bi, ti, j: (bi, jnp.maximum(ti * tpb - 1, 0), j))
    return pl.pallas_call(
        functools.partial(_ffn_act_kernel, from_state=from_state),
        grid=(b, t // tt, nf),
        in_specs=[pl.BlockSpec((None, tt, tf), lambda bi, ti, j: (bi, ti, j)),
                  prev_spec,
                  pl.BlockSpec((None, tt, tf), lambda bi, ti, j: (bi, ti, nf + j)),
                  pl.BlockSpec((ROW_TILE, tf), lambda bi, ti, j: (0, j)),
                  pl.BlockSpec((1, tf), lambda bi, ti, j: (0, j))],
        out_specs=pl.BlockSpec((None, tt, tf), lambda bi, ti, j: (bi, ti, j)),
        out_shape=jax.ShapeDtypeStruct((b, t, d_ff), BF16),
        scratch_shapes=[pltpu.VMEM((ROW_TILE + tt, tf), F32)],
        compiler_params=_cparams("parallel", "parallel", "parallel"),
        name="ffn_act",
    )(gu, prev_arr, gu, wconv8, b_conv.reshape(1, d_ff))


def _down_norm_kernel(a_ref, w_ref, r_ref, g_ref, o_ref, acc_ref):
    kk = pl.program_id(1)

    @pl.when(kk == 0)
    def _():
        acc_ref[...] = r_ref[...]

    acc_ref[...] += _dot(a_ref[...], w_ref[...])

    @pl.when(kk == pl.num_programs(1) - 1)
    def _():
        hh = acc_ref[...]
        o_ref[...] = hh * lax.rsqrt(jnp.mean(hh * hh, axis=-1, keepdims=True) + EPS) * g_ref[...]


def _down_norm(a, w, resid, g, *, tm, tk):
    m, k = a.shape
    n = w.shape[1]
    return pl.pallas_call(
        _down_norm_kernel,
        grid=(m // tm, k // tk),
        in_specs=[pl.BlockSpec((tm, tk), lambda i, kk: (i, kk)),
                  pl.BlockSpec((tk, n), lambda i, kk: (kk, 0)),
                  pl.BlockSpec((tm, n), lambda i, kk: (i, 0)),
                  pl.BlockSpec((1, n), lambda i, kk: (0, 0))],
        out_specs=pl.BlockSpec((tm, n), lambda i, kk: (i, 0)),
        out_shape=jax.ShapeDtypeStruct((m, n), F32),
        scratch_shapes=[pltpu.VMEM((tm, n), F32)],
        compiler_params=_cparams("parallel", "arbitrary"),
        name="down_norm",
    )(a, w, resid, g.reshape(1, n))


def _row_tile(rows, cap):
    tile = rows
    while tile > cap:
        assert tile % 2 == 0
        tile //= 2
    return tile


def _trunk(x, attend, gdn_state, ffn_state8, wts, *, dims, chunk, t_valid):
    b, t, d = x.shape
    m = b * t
    hb, dk, d_ff = dims["hb"], dims["dk"], dims["d_ff"]
    wa_width = dims["width_a"]
    x2 = x.reshape(m, d)
    tm = _row_tile(m, 1024)
    tm_acc = _row_tile(m, 512)
    proj, small = _norm_matmul(x2, wts["w_norm_mix"], wts["w_in_main"], wts["w_in_small"], tm=tm, tn=1024)
    cols = proj.shape[1]
    proj3 = proj.reshape(b, t, cols)
    small3 = small.reshape(b, t, LANE)
    o_a = attend(proj3)
    conv_state8, s0 = gdn_state
    wv = hb * dk
    o_b, ssm_new = _gdn(proj3, small3, wts["w_conv_gdn"], wts["a_log"], wts["dt_bias"], wts["w_norm_gdn"],
                        conv_state8, s0, n_heads=hb, dk=dk, chunk=chunk, t_valid=t_valid,
                        qkv_block=(3 * wa_width) // (3 * wv), z_block=(3 * wa_width + 3 * wv) // wv)
    gate_a_col = 3 * wa_width + 4 * wv
    merged = _merge(o_a.reshape(m, wa_width), o_b.reshape(m, wv), proj, wts["w_branch_a"], wts["w_branch_b"],
                    gate_a_col=gate_a_col, gate_b_col=gate_a_col + d, tm=tm, tn=1024)
    h = _matmul_res(merged, wts["w_out"], x2, tm=tm, tn=1024)
    gu = _norm_matmul(h, wts["w_norm_ffn"], wts["w_ffn_in"], tm=tm, tn=1024)
    gu3 = gu.reshape(b, t, 2 * d_ff)
    hidden = _ffn_act(gu3, wts["w_ffn_conv"], wts["b_ffn_conv"], ffn_state8, d_ff=d_ff,
                      tt=_row_tile(t, 512), tf=d_ff // 4)
    y = _down_norm(hidden.reshape(m, d_ff), wts["w_ffn_down"], h, wts["w_norm_final"], tm=tm_acc, tk=d_ff // 4)
    return y.reshape(b, t, d), proj3, gu3, ssm_new


def kernel(x_prompt, x_sample, cache_k, cache_v, state_conv, state_ssm, state_ffn_conv, page_table, w_norm_mix, w_in, lambda_q1, lambda_k1, lambda_q2, lambda_k2, w_subln, w_conv_gdn, a_log, dt_bias, w_norm_gdn, w_branch_a, w_branch_b, w_out, w_norm_ffn, w_ffn_in, w_ffn_conv, b_ffn_conv, w_ffn_down, w_norm_final):
    depth = w_in.shape[0]
    assert depth == 1, "single-layer trunk"
    l = 0
    bp, seq, d = x_prompt.shape
    bs, dec_seq, _ = x_sample.shape
    ha = cache_k.shape[3]
    width_a = ha * cache_k.shape[4]
    hb, dk = state_ssm.shape[2], state_ssm.shape[3]
    wv = hb * dk
    d_ff = w_ffn_conv.shape[2]
    n_pool, page = cache_k.shape[1], cache_k.shape[2]
    dims = dict(hb=hb, dk=dk, d_ff=d_ff, width_a=width_a)
    lam_init = 0.8 - 0.6 * math.exp(-0.3 * l)
    slopes = 2.0 ** (-8.0 * jnp.arange(1, ha + 1, dtype=F32) / ha)

    c_small = 3 * width_a + 4 * wv
    w_in_l = w_in[l]
    wts = dict(
        w_norm_mix=w_norm_mix[l],
        w_in_main=jnp.concatenate([w_in_l[:, :c_small], w_in_l[:, c_small + 2 * hb:]], axis=1).astype(BF16),
        w_in_small=jnp.pad(w_in_l[:, c_small:c_small + 2 * hb], ((0, 0), (0, LANE - 2 * hb))).astype(BF16),
        w_conv_gdn=w_conv_gdn[l], a_log=a_log[l], dt_bias=dt_bias[l], w_norm_gdn=w_norm_gdn[l],
        w_branch_a=w_branch_a[l].astype(BF16), w_branch_b=w_branch_b[l].astype(BF16),
        w_out=w_out[l].astype(BF16), w_norm_ffn=w_norm_ffn[l], w_ffn_in=w_ffn_in[l].astype(BF16),
        w_ffn_conv=w_ffn_conv[l], b_ffn_conv=b_ffn_conv[l], w_ffn_down=w_ffn_down[l].astype(BF16),
        w_norm_final=w_norm_final,
    )
    lam_params = jnp.zeros((ROW_TILE, LANE), F32)
    lam_params = (lam_params.at[0, :HEAD_DIM_A].set(lambda_q1[l]).at[1, :HEAD_DIM_A].set(lambda_k1[l])
                  .at[2, :HEAD_DIM_A].set(lambda_q2[l]).at[3, :HEAD_DIM_A].set(lambda_k2[l]))

    attend_p = functools.partial(_attn_prompt, slopes=slopes, lam_params=lam_params, w_subln=w_subln[l],
                                 n_heads=ha, lam_init=lam_init, tq=512)
    y_p, proj_p, gu_p, ssm_p = _trunk(x_prompt, attend_p, (None, None), None, wts, dims=dims,
                                      chunk=GDN_CHUNK, t_valid=GDN_CHUNK)

    assert dec_seq >= GDN_CONV - 1 and dec_seq >= FFN_CONV - 1
    pad_t = SAMPLE_ROWS - dec_seq
    x_s = jnp.pad(x_sample, ((0, 0), (0, pad_t), (0, 0)))
    conv_state8 = jnp.pad(state_conv[l], ((0, 0), (ROW_TILE - (GDN_CONV - 1), 0), (0, 0)))
    ffn_state8 = jnp.pad(state_ffn_conv[l], ((0, 0), (ROW_TILE - (FFN_CONV - 1), 0), (0, 0)))
    ck = cache_k.reshape(depth * n_pool, page, ha, width_a // ha)
    cv = cache_v.reshape(depth * n_pool, page, ha, width_a // ha)
    attend_s = functools.partial(_attn_sample, cache_k=ck, cache_v=cv, page_table=page_table + l * n_pool,
                                 slopes=slopes, lam_params=lam_params, w_subln=w_subln[l], n_heads=ha,
                                 dec_seq=dec_seq, lam_init=lam_init, n_group=8)
    y_s, proj_s, gu_s, ssm_s = _trunk(x_s, attend_s, (conv_state8, state_ssm[l]), ffn_state8, wts, dims=dims,
                                      chunk=SAMPLE_ROWS, t_valid=dec_seq)

    def outputs(y, proj, gu, ssm, t_real):
        bb = y.shape[0]
        k_rows = proj[:, :t_real, width_a:2 * width_a].reshape(1, bb, t_real, ha, width_a // ha)
        v_rows = proj[:, :t_real, 2 * width_a:3 * width_a].reshape(1, bb, t_real, ha, width_a // ha)
        conv_new = proj[:, t_real - (GDN_CONV - 1):t_real, 3 * width_a:3 * width_a + 3 * wv][None]
        ffn_new = gu[:, t_real - (FFN_CONV - 1):t_real, :d_ff][None]
        return y[:, :t_real], k_rows, v_rows, conv_new, ssm[None], ffn_new

    yp, kp, vp, cp, sp, fp = outputs(y_p, proj_p, gu_p, ssm_p, seq)
    ys, ks, vs, cs, ss, fs = outputs(y_s, proj_s, gu_s, ssm_s, dec_seq)
    return (yp, ys, kp, vp, cp, sp, fp, ks, vs, cs, ss, fs)
```

```python
import functools
import math

import jax
import jax.numpy as jnp
from jax import lax
from jax.experimental import pallas as pl
from jax.experimental.pallas import tpu as pltpu

F32 = jnp.float32
BF16 = jnp.bfloat16
EPS = 1e-6
NEG = -1e30
LOG2E = math.log2(math.e)
ROW_TILE = 8
SAMPLE_ROWS = 16
LANE = 128
VMEM_LIMIT = 56 * 1024 * 1024

HEAD_DIM_A = 64
GDN_CONV = 4
GDN_CHUNK = 64
FFN_CONV = 3


def _cparams(*sem):
    return pltpu.CompilerParams(dimension_semantics=sem, vmem_limit_bytes=VMEM_LIMIT)


def _dot(a, b):
    return jnp.dot(a, b, preferred_element_type=F32)


def _dot_nt(a, b):
    return lax.dot_general(a, b, (((1,), (1,)), ((), ())), preferred_element_type=F32)


def _dot_tn(a, b):
    return lax.dot_general(a, b, (((0,), (0,)), ((), ())), preferred_element_type=F32)


def _sigmoid(x):
    return 0.5 * jnp.tanh(0.5 * x) + 0.5


def _silu(x):
    return x * _sigmoid(x)


def _norm_matmul_kernel(*refs, has_small, norm_rows):
    if has_small:
        x_ref, g_ref, w_ref, w2_ref, o_ref, o2_ref, xn_ref = refs
    else:
        x_ref, g_ref, w_ref, o_ref, xn_ref = refs

    @pl.when(pl.program_id(1) == 0)
    def _():
        def body(r, carry):
            rows = pl.ds(pl.multiple_of(r * norm_rows, norm_rows), norm_rows)
            x = x_ref[rows, :]
            xn = x * lax.rsqrt(jnp.mean(x * x, axis=-1, keepdims=True) + EPS) * g_ref[...]
            xn_ref[rows, :] = xn.astype(BF16)
            return carry
        lax.fori_loop(0, x_ref.shape[0] // norm_rows, body, 0)
        if has_small:
            o2_ref[...] = _dot(xn_ref[...], w2_ref[...])

    o_ref[...] = _dot(xn_ref[...], w_ref[...])


def _norm_matmul(x, g, w, w2=None, *, tm, tn):
    m, k = x.shape
    n = w.shape[1]
    has_small = w2 is not None
    in_specs = [pl.BlockSpec((tm, k), lambda i, j: (i, 0)),
                pl.BlockSpec((1, k), lambda i, j: (0, 0)),
                pl.BlockSpec((k, tn), lambda i, j: (0, j))]
    out_specs = [pl.BlockSpec((tm, tn), lambda i, j: (i, j))]
    out_shape = [jax.ShapeDtypeStruct((m, n), F32)]
    args = [x, g.reshape(1, k), w]
    if has_small:
        in_specs.append(pl.BlockSpec((k, LANE), lambda i, j: (0, 0)))
        out_specs.append(pl.BlockSpec((tm, LANE), lambda i, j: (i, 0)))
        out_shape.append(jax.ShapeDtypeStruct((m, LANE), F32))
        args.append(w2)
    outs = pl.pallas_call(
        functools.partial(_norm_matmul_kernel, has_small=has_small, norm_rows=min(tm, 64)),
        grid=(m // tm, n // tn),
        in_specs=in_specs, out_specs=out_specs, out_shape=out_shape,
        scratch_shapes=[pltpu.VMEM((tm, k), BF16)],
        compiler_params=_cparams("parallel", "arbitrary"),
        name="norm_matmul",
    )(*args)
    return outs if has_small else outs[0]


def _lambda_value(lam_ref, lam_init):
    p = lam_ref[...]
    t1 = jnp.sum(p[0:1, :] * p[1:2, :], axis=-1, keepdims=True)
    t2 = jnp.sum(p[2:3, :] * p[3:4, :], axis=-1, keepdims=True)
    return jnp.exp(t1) - jnp.exp(t2) + lam_init


def _subln(o, w, lam_init):
    n = o * lax.rsqrt(jnp.mean(o * o, axis=-1, keepdims=True) + EPS) * w
    return n * (1.0 - lam_init)


def _attn_prompt_kernel(slope_ref, q_ref, k_ref, v_ref, rel_ref, lam_ref, wsub_ref, o_ref,
                        kb_sc, vt_sc, bias_sc, q2_sc, m_sc, l_sc, acc_sc, *, tq, qcols, lam_init):
    h = pl.program_id(1)
    qi = pl.program_id(2)

    slope2 = slope_ref[h] * LOG2E

    @pl.when(qi == 0)
    def _():
        kb_sc[...] = k_ref[...].astype(BF16)
        vt_sc[...] = v_ref[...].T.astype(BF16)
        bias_sc[...] = slope2 * rel_ref[...]

    q = q_ref[...] * (HEAD_DIM_A ** -0.5 * LOG2E)
    lane = lax.broadcasted_iota(jnp.int32, (1, LANE), 1)
    q2_sc[0:tq, :] = jnp.where(lane < HEAD_DIM_A, q, 0.0).astype(BF16)
    q2_sc[tq:2 * tq, :] = jnp.where(lane >= HEAD_DIM_A, q, 0.0).astype(BF16)
    m_sc[...] = jnp.full(m_sc.shape, NEG, F32)
    l_sc[...] = jnp.zeros(l_sc.shape, F32)
    acc_sc[...] = jnp.zeros(acc_sc.shape, F32)

    def key_block(ki, masked):
        k0 = pl.multiple_of(ki * tq, tq)
        kt = kb_sc[pl.ds(k0, tq), :]
        vt = vt_sc[:, pl.ds(k0, tq)]
        shift = slope2 * ((ki - qi) * tq).astype(F32)
        cols = [pl.ds(j * qcols, qcols) for j in range(2 * tq // qcols)]
        scores = [_dot_nt(kt, q2_sc[c, :]) for c in cols]
        for c, s in zip(cols, scores):
            bias = bias_sc[:, c]
            if masked:
                bias = jnp.where(rel_ref[:, c] <= 0.0, bias, NEG)
            s = s + bias
            m_prev = m_sc[:, c]
            m_new = jnp.maximum(m_prev, jnp.max(s, axis=0, keepdims=True) + shift)
            alpha = jnp.exp2(m_prev - m_new)
            p = jnp.exp2(s - (m_new - shift))
            l_sc[:, c] = alpha * l_sc[:, c] + jnp.sum(p, axis=0, keepdims=True)
            acc_sc[:, c] = alpha * acc_sc[:, c] + _dot(vt, p.astype(BF16))
            m_sc[:, c] = m_new

    def body(ki, carry):
        key_block(ki, False)
        return carry

    lax.fori_loop(0, qi, body, 0)
    key_block(qi, True)

    lam = _lambda_value(lam_ref, lam_init)
    n = acc_sc[...] / l_sc[...]
    o = (n[:, 0:tq] - lam * n[:, tq:2 * tq]).T
    o_ref[...] = _subln(o, wsub_ref[...], lam_init).astype(o_ref.dtype)


def _attn_prompt(proj, slopes, lam_params, w_subln, *, n_heads, lam_init, tq):
    b, t, _ = proj.shape
    rel = (jnp.arange(tq, dtype=jnp.int32)[:, None] - jnp.arange(tq, dtype=jnp.int32)[None, :]).astype(F32)
    rel = jnp.concatenate([rel, rel], axis=1)
    hh = n_heads
    grid_spec = pltpu.PrefetchScalarGridSpec(
        num_scalar_prefetch=0,
        grid=(b, hh, t // tq),
        in_specs=[
            pl.BlockSpec(memory_space=pltpu.SMEM),
            pl.BlockSpec((None, tq, LANE), lambda bi, h, qi: (bi, qi, h)),
            pl.BlockSpec((None, t, LANE), lambda bi, h, qi: (bi, 0, hh + h)),
            pl.BlockSpec((None, t, LANE), lambda bi, h, qi: (bi, 0, 2 * hh + h)),
            pl.BlockSpec((tq, 2 * tq), lambda bi, h, qi: (0, 0)),
            pl.BlockSpec((ROW_TILE, LANE), lambda bi, h, qi: (0, 0)),
            pl.BlockSpec((1, LANE), lambda bi, h, qi: (0, 0)),
        ],
        out_specs=pl.BlockSpec((None, tq, LANE), lambda bi, h, qi: (bi, qi, h)),
        scratch_shapes=[pltpu.VMEM((t, LANE), BF16), pltpu.VMEM((LANE, t), BF16),
                        pltpu.VMEM((tq, 2 * tq), F32),
                        pltpu.VMEM((2 * tq, LANE), BF16), pltpu.VMEM((1, 2 * tq), F32),
                        pltpu.VMEM((1, 2 * tq), F32), pltpu.VMEM((LANE, 2 * tq), F32)],
    )
    return pl.pallas_call(
        functools.partial(_attn_prompt_kernel, tq=tq, qcols=2 * tq, lam_init=lam_init),
        grid_spec=grid_spec,
        out_shape=jax.ShapeDtypeStruct((b, t, n_heads * LANE), BF16),
        compiler_params=_cparams("parallel", "parallel", "arbitrary"),
        name="attn_prompt",
    )(slopes, proj, proj, proj, rel, lam_params, w_subln.reshape(1, LANE))


def _attn_sample_kernel(pt_ref, q_ref, kn_ref, vn_ref, slope_ref, trow_ref, lam_ref, wsub_ref, *rest,
                        n_group, n_heads, dec_seq, past_len, page, lam_init):
    k_refs = rest[:n_group]
    v_refs = rest[n_group:2 * n_group]
    o_ref, qm_sc, bias_sc, m_sc, l_sc, acc_sc = rest[2 * n_group:]
    p_id = pl.program_id(1)
    hs = n_heads * ROW_TILE
    hshift = int(math.log2(n_heads))
    slope2 = slope_ref[:, 0:1] * LOG2E
    trow = trow_ref[:, 0:1]

    def col_info(n_cols):
        row = lax.broadcasted_iota(jnp.int32, (hs, n_cols), 0)
        col = lax.broadcasted_iota(jnp.int32, (hs, n_cols), 1)
        same_head = (col & (n_heads - 1)) == (row >> int(math.log2(ROW_TILE)))
        return same_head, (col >> hshift).astype(F32)

    @pl.when(p_id == 0)
    def _():
        q = q_ref[...] * (HEAD_DIM_A ** -0.5 * LOG2E)
        row = lax.broadcasted_iota(jnp.int32, (hs, LANE), 0)
        lane = lax.broadcasted_iota(jnp.int32, (hs, LANE), 1)
        second_map = ((row >> int(math.log2(dec_seq))) & 1) == 1
        qm = jnp.where((lane >= HEAD_DIM_A) == second_map, q, 0.0).astype(BF16)
        qm_sc[...] = qm
        same_head, key = col_info(page * n_heads)
        bias_sc[...] = jnp.where(same_head, slope2 * key, NEG)
        same_head, key = col_info(ROW_TILE * n_heads)
        dist = trow - key
        s = _dot_nt(qm, kn_ref[...].astype(BF16)) - slope2 * dist
        s = jnp.where(same_head & (dist >= 0.0) & (key < dec_seq), s, NEG)
        m0 = jnp.max(s, axis=-1, keepdims=True)
        p = jnp.exp2(s - m0)
        m_sc[...] = m0
        l_sc[...] = jnp.sum(p, axis=-1, keepdims=True)
        acc_sc[...] = _dot(p.astype(BF16), vn_ref[...].astype(BF16))

    qm = qm_sc[...]
    bias = bias_sc[...]
    scores, shifts = [], []
    for g in range(n_group):
        kg = k_refs[g][...].reshape(page * n_heads, LANE).astype(BF16)
        scores.append(_dot_nt(qm, kg) + bias)
        kpos0 = ((p_id * n_group + g) * page).astype(F32)
        shifts.append(slope2 * ((past_len + trow) - kpos0))
    m_prev = m_sc[...]
    m_new = m_prev
    for s, sh in zip(scores, shifts):
        m_new = jnp.maximum(m_new, jnp.max(s, axis=-1, keepdims=True) - sh)
    alpha = jnp.exp2(m_prev - m_new)
    l_new = alpha * l_sc[...]
    acc = alpha * acc_sc[...]
    for g in range(n_group):
        p = jnp.exp2(scores[g] - (m_new + shifts[g]))
        l_new = l_new + jnp.sum(p, axis=-1, keepdims=True)
        vg = v_refs[g][...].reshape(page * n_heads, LANE).astype(BF16)
        acc = acc + _dot(p.astype(BF16), vg)
    m_sc[...] = m_new
    l_sc[...] = l_new
    acc_sc[...] = acc

    @pl.when(p_id == pl.num_programs(1) - 1)
    def _():
        lam = _lambda_value(lam_ref, lam_init)
        n = acc_sc[...] / l_sc[...]
        o = n - lam * pltpu.roll(n, hs - dec_seq, axis=0)
        o = _subln(o, wsub_ref[...], lam_init)
        pad = jnp.zeros((o_ref.shape[0] - ROW_TILE, LANE), F32)
        for hd in range(n_heads):
            rows = jnp.concatenate([o[hd * ROW_TILE:(hd + 1) * ROW_TILE], pad], axis=0)
            o_ref[:, hd * LANE:(hd + 1) * LANE] = rows.astype(o_ref.dtype)


def _attn_sample(proj, cache_k, cache_v, page_table, slopes, lam_params, w_subln, *,
                 n_heads, dec_seq, lam_init, n_group):
    assert 2 * dec_seq == ROW_TILE
    b, t_pad, _ = proj.shape
    n_pages = page_table.shape[1]
    page = cache_k.shape[1]
    width = n_heads * LANE
    hs = n_heads * ROW_TILE
    past_len = n_pages * page
    q4 = proj[:, :dec_seq, :width].reshape(b, dec_seq, n_heads, LANE).transpose(0, 2, 1, 3)
    q_rows = jnp.concatenate([q4, q4], axis=2).reshape(b, hs, LANE)
    kn_rows = proj[:, :ROW_TILE, width:2 * width].reshape(b, ROW_TILE * n_heads, LANE)
    vn_rows = proj[:, :ROW_TILE, 2 * width:3 * width].reshape(b, ROW_TILE * n_heads, LANE)
    rows = jnp.arange(hs)
    slope_rows = jnp.broadcast_to(slopes[rows // ROW_TILE][:, None], (hs, LANE)).astype(F32)
    trow = jnp.broadcast_to((rows % dec_seq).astype(F32)[:, None], (hs, LANE))
    const = lambda bi, p, pt: (0, 0)

    def page_spec(g):
        return pl.BlockSpec((None, page, n_heads, LANE), lambda bi, p, pt: (pt[bi, p * n_group + g], 0, 0, 0))

    grid_spec = pltpu.PrefetchScalarGridSpec(
        num_scalar_prefetch=1,
        grid=(b, n_pages // n_group),
        in_specs=[
            pl.BlockSpec((None, hs, LANE), lambda bi, p, pt: (bi, 0, 0)),
            pl.BlockSpec((None, ROW_TILE * n_heads, LANE), lambda bi, p, pt: (bi, 0, 0)),
            pl.BlockSpec((None, ROW_TILE * n_heads, LANE), lambda bi, p, pt: (bi, 0, 0)),
            pl.BlockSpec((hs, LANE), const),
            pl.BlockSpec((hs, LANE), const),
            pl.BlockSpec((ROW_TILE, LANE), const),
            pl.BlockSpec((1, LANE), const),
        ] + [page_spec(g) for g in range(n_group)] + [page_spec(g) for g in range(n_group)],
        out_specs=pl.BlockSpec((None, t_pad, width), lambda bi, p, pt: (bi, 0, 0)),
        scratch_shapes=[pltpu.VMEM((hs, LANE), BF16), pltpu.VMEM((hs, page * n_heads), F32),
                        pltpu.VMEM((hs, 1), F32), pltpu.VMEM((hs, 1), F32), pltpu.VMEM((hs, LANE), F32)],
    )
    return pl.pallas_call(
        functools.partial(_attn_sample_kernel, n_group=n_group, n_heads=n_heads, dec_seq=dec_seq,
                          past_len=float(past_len), page=page, lam_init=lam_init),
        grid_spec=grid_spec,
        out_shape=jax.ShapeDtypeStruct((b, t_pad, width), BF16),
        compiler_params=_cparams("parallel", "arbitrary"),
        name="attn_sample",
    )(page_table, q_rows, kn_rows, vn_rows, slope_rows, trow, lam_params, w_subln.reshape(1, LANE),
      *([cache_k] * n_group), *([cache_v] * n_group))


def _split3_dot(a_bf16, x):
    x1 = x.astype(BF16)
    r1 = x - x1.astype(F32)
    x2 = r1.astype(BF16)
    x3 = (r1 - x2.astype(F32)).astype(BF16)
    return _dot(a_bf16, x1) + _dot(a_bf16, x2) + _dot(a_bf16, x3)


def _gdn_kernel(*refs, chunk, t_valid, n_heads, dk, has_state):
    if has_state:
        (x_ref, z_ref, ba_ref, wconv_ref, prm_ref, wnorm_ref, cstate_ref, s0_ref,
         o_ref, sout_ref, xbuf, s_sc) = refs
    else:
        (x_ref, z_ref, ba_ref, wconv_ref, prm_ref, wnorm_ref,
         o_ref, sout_ref, xbuf, s_sc) = refs
    n = pl.program_id(1)
    c = chunk
    wqk = n_heads * dk

    @pl.when(n == 0)
    def _():
        if has_state:
            xbuf[0:ROW_TILE, :] = cstate_ref[...]
            s_sc[...] = s0_ref[...]
        else:
            xbuf[0:ROW_TILE, :] = jnp.zeros((ROW_TILE, xbuf.shape[1]), F32)
            s_sc[...] = jnp.zeros(s_sc.shape, F32)

    xbuf[ROW_TILE:ROW_TILE + c, :] = x_ref[...]
    xfull = xbuf[...]
    y = xfull[ROW_TILE:] * wconv_ref[GDN_CONV - 1:GDN_CONV, :]
    for d in range(1, GDN_CONV):
        y = y + pltpu.roll(xfull, d, axis=0)[ROW_TILE:] * wconv_ref[GDN_CONV - 1 - d:GDN_CONV - d, :]
    if c > ROW_TILE:
        xbuf[0:ROW_TILE, :] = xfull[c:c + ROW_TILE]
    act = _silu(y)

    ba = ba_ref[...]
    beta_all = _sigmoid(ba)
    sp_in = ba + prm_ref[1:2, :]
    softplus = jnp.maximum(sp_in, 0.0) + jnp.log(1.0 + jnp.exp(-jnp.abs(sp_in)))
    g_all = -jnp.exp(prm_ref[0:1, :]) * softplus
    if t_valid < c:
        rv = lax.broadcasted_iota(jnp.int32, (c, 1), 0) < t_valid
        act = jnp.where(rv, act, 0.0)
        beta_all = jnp.where(rv, beta_all, 0.0)
        g_all = jnp.where(rv, g_all, 0.0)

    ri = lax.broadcasted_iota(jnp.int32, (c, c), 0)
    ci = lax.broadcasted_iota(jnp.int32, (c, c), 1)
    strict = ri > ci
    incl = ri >= ci
    eye = (ri == ci).astype(F32)
    tri = incl.astype(BF16)
    gc_all = _split3_dot(tri, g_all)

    heads = range(n_heads)
    n_double = int(math.log2(c)) - 1
    qs, ks, vs = [], [], []
    for h in heads:
        q = act[:, h * dk:(h + 1) * dk]
        k = act[:, wqk + h * dk:wqk + (h + 1) * dk]
        qs.append(q * lax.rsqrt(jnp.sum(q * q, axis=-1, keepdims=True) + EPS) * (dk ** -0.5))
        ks.append(k * lax.rsqrt(jnp.sum(k * k, axis=-1, keepdims=True) + EPS))
        vs.append(act[:, 2 * wqk + h * dk:2 * wqk + (h + 1) * dk])
    betas = [beta_all[:, h:h + 1] for h in heads]
    gcs = [gc_all[:, n_heads + h:n_heads + h + 1] for h in heads]
    decays = [jnp.exp(jnp.where(incl, gc - jnp.sum(eye * gc, axis=0, keepdims=True), NEG)) for gc in gcs]
    kbs = [k * b for k, b in zip(ks, betas)]
    kbf = [k.astype(BF16) for k in ks]
    kq = [_dot_nt(jnp.concatenate([kb.astype(BF16), q.astype(BF16)], axis=0), kf)
          for kb, q, kf in zip(kbs, qs, kbf)]
    xps = [jnp.where(strict, -(m[:c] * d), 0.0) for m, d in zip(kq, decays)]
    tinvs = [eye + x for x in xps]
    xps = [_dot(x.astype(BF16), x.astype(BF16)) for x in xps]
    for i in range(n_double):
        xpb = [x.astype(BF16) for x in xps]
        tinvs = [t + _dot(xb, t.astype(BF16)) for t, xb in zip(tinvs, xpb)]
        if i < n_double - 1:
            xps = [_dot(xb, xb) for xb in xpb]
    egcs = [jnp.exp(gc) for gc in gcs]
    uw = [_dot(t.astype(BF16), jnp.concatenate([(v * b).astype(BF16), (kb * e).astype(BF16)], axis=1))
          for t, v, b, kb, e in zip(tinvs, vs, betas, kbs, egcs)]
    g_lasts = [gc[c - 1:c, :] for gc in gcs]
    k_decs = [(k * jnp.exp(gl - gc)).astype(BF16) for k, gl, gc in zip(ks, g_lasts, gcs)]
    states = [s_sc[h] for h in heads]
    ws_qs = [_dot(jnp.concatenate([m[:, dk:].astype(BF16), (q * e).astype(BF16)], axis=0), s.astype(BF16))
             for m, q, e, s in zip(uw, qs, egcs, states)]
    v_newb = [(m[:, :dk] - x[:c]).astype(BF16) for m, x in zip(uw, ws_qs)]
    outs = [x[c:] + _dot((m[c:] * d).astype(BF16), vn) for x, m, d, vn in zip(ws_qs, kq, decays, v_newb)]
    for h in heads:
        s_sc[h] = states[h] * jnp.exp(g_lasts[h]) + _dot_tn(k_decs[h], v_newb[h])
    for h in heads:
        o = outs[h]
        on = o * lax.rsqrt(jnp.mean(o * o, axis=-1, keepdims=True) + EPS) * wnorm_ref[...]
        zz = z_ref[:, h * dk:(h + 1) * dk]
        o_ref[:, h * dk:(h + 1) * dk] = (on * _silu(zz)).astype(o_ref.dtype)

    @pl.when(n == pl.num_programs(1) - 1)
    def _():
        sout_ref[...] = s_sc[...]


def _gdn(proj, small, w_conv, a_log, dt_bias, w_norm, conv_state8, s0, *, n_heads, dk, chunk, t_valid,
         qkv_block, z_block):
    b, t, _ = proj.shape
    wv = n_heads * dk
    cch = 3 * wv
    has_state = s0 is not None
    wconv8 = jnp.zeros((ROW_TILE, cch), F32).at[:GDN_CONV].set(w_conv)
    prm = jnp.zeros((ROW_TILE, LANE), F32)
    prm = prm.at[0, n_heads:2 * n_heads].set(a_log).at[1, n_heads:2 * n_heads].set(dt_bias)
    const = lambda bi, n: (0, 0)
    in_specs = [
        pl.BlockSpec((None, chunk, cch), lambda bi, n: (bi, n, qkv_block)),
        pl.BlockSpec((None, chunk, wv), lambda bi, n: (bi, n, z_block)),
        pl.BlockSpec((None, chunk, LANE), lambda bi, n: (bi, n, 0)),
        pl.BlockSpec((ROW_TILE, cch), const),
        pl.BlockSpec((ROW_TILE, LANE), const),
        pl.BlockSpec((1, dk), const),
    ]
    args = [proj, proj, small, wconv8, prm, w_norm.reshape(1, dk)]
    if has_state:
        in_specs += [pl.BlockSpec((None, ROW_TILE, cch), lambda bi, n: (bi, 0, 0)),
                     pl.BlockSpec((None, n_heads, dk, dk), lambda bi, n: (bi, 0, 0, 0))]
        args += [conv_state8, s0]
    o, s_out = pl.pallas_call(
        functools.partial(_gdn_kernel, chunk=chunk, t_valid=t_valid, n_heads=n_heads, dk=dk,
                          has_state=has_state),
        grid=(b, t // chunk),
        in_specs=in_specs,
        out_specs=[pl.BlockSpec((None, chunk, wv), lambda bi, n: (bi, n, 0)),
                   pl.BlockSpec((None, n_heads, dk, dk), lambda bi, n: (bi, 0, 0, 0))],
        out_shape=[jax.ShapeDtypeStruct((b, t, wv), BF16),
                   jax.ShapeDtypeStruct((b, n_heads, dk, dk), F32)],
        scratch_shapes=[pltpu.VMEM((ROW_TILE + chunk, cch), F32), pltpu.VMEM((n_heads, dk, dk), F32)],
        compiler_params=_cparams("parallel", "arbitrary"),
        name="gdn",
    )(*args)
    return o, s_out


def _merge_kernel(oa_ref, ob_ref, ga_ref, gb_ref, wa_ref, wb_ref, o_ref):
    pa = _dot(oa_ref[...], wa_ref[...])
    pb = _dot(ob_ref[...], wb_ref[...])
    o_ref[...] = (_sigmoid(ga_ref[...]) * pa + _sigmoid(gb_ref[...]) * pb).astype(o_ref.dtype)


def _merge(o_a, o_b, proj, w_a, w_b, *, gate_a_col, gate_b_col, tm, tn):
    m, ka = o_a.shape
    n = w_a.shape[1]
    return pl.pallas_call(
        _merge_kernel,
        grid=(m // tm, n // tn),
        in_specs=[pl.BlockSpec((tm, ka), lambda i, j: (i, 0)),
                  pl.BlockSpec((tm, o_b.shape[1]), lambda i, j: (i, 0)),
                  pl.BlockSpec((tm, tn), lambda i, j: (i, gate_a_col // tn + j)),
                  pl.BlockSpec((tm, tn), lambda i, j: (i, gate_b_col // tn + j)),
                  pl.BlockSpec((ka, tn), lambda i, j: (0, j)),
                  pl.BlockSpec((w_b.shape[0], tn), lambda i, j: (0, j))],
        out_specs=pl.BlockSpec((tm, tn), lambda i, j: (i, j)),
        out_shape=jax.ShapeDtypeStruct((m, n), BF16),
        compiler_params=_cparams("parallel", "arbitrary"),
        name="merge",
    )(o_a, o_b, proj, proj, w_a, w_b)


def _matmul_res_kernel(a_ref, w_ref, r_ref, o_ref):
    o_ref[...] = r_ref[...] + _dot(a_ref[...], w_ref[...])


def _matmul_res(a, w, resid, *, tm, tn):
    m, k = a.shape
    n = w.shape[1]
    return pl.pallas_call(
        _matmul_res_kernel,
        grid=(m // tm, n // tn),
        in_specs=[pl.BlockSpec((tm, k), lambda i, j: (i, 0)),
                  pl.BlockSpec((k, tn), lambda i, j: (0, j)),
                  pl.BlockSpec((tm, tn), lambda i, j: (i, j))],
        out_specs=pl.BlockSpec((tm, tn), lambda i, j: (i, j)),
        out_shape=jax.ShapeDtypeStruct((m, n), F32),
        compiler_params=_cparams("parallel", "arbitrary"),
        name="matmul_res",
    )(a, w, resid)


def _rmsnorm_rows(x, g):
    return (x * lax.rsqrt(jnp.mean(x * x, axis=-1, keepdims=True) + EPS) * g).astype(BF16)


def _ffn_in_kernel(x_ref, xprev_ref, g_ref, wg_ref, wu_ref, wconv_ref, b_ref, o_ref, tail_ref,
                   xn_sc, xp_sc, buf, *, norm_rows, tiles_per_seq):
    tm = x_ref.shape[0]

    @pl.when(pl.program_id(1) == 0)
    def _():
        def body(r, carry):
            rows = pl.ds(pl.multiple_of(r * norm_rows, norm_rows), norm_rows)
            xn_sc[rows, :] = _rmsnorm_rows(x_ref[rows, :], g_ref[...])
            return carry
        lax.fori_loop(0, tm // norm_rows, body, 0)
        xp_sc[...] = _rmsnorm_rows(xprev_ref[...], g_ref[...])

    gate = _dot(xn_sc[...], wg_ref[...])
    up = _dot(xn_sc[...], wu_ref[...])
    prev = _dot(xp_sc[...], wg_ref[...])[xprev_ref.shape[0] - ROW_TILE:]
    prev = jnp.where(pl.program_id(0) % tiles_per_seq > 0, prev, 0.0)
    tail_ref[...] = gate[tm - ROW_TILE:]
    buf[0:ROW_TILE, :] = prev
    buf[ROW_TILE:ROW_TILE + tm, :] = gate
    full = buf[...]
    y = full[ROW_TILE:] * wconv_ref[FFN_CONV - 1:FFN_CONV, :] + b_ref[...]
    for d in range(1, FFN_CONV):
        y = y + pltpu.roll(full, d, axis=0)[ROW_TILE:] * wconv_ref[FFN_CONV - 1 - d:FFN_CONV - d, :]
    o_ref[...] = (_silu(y) * up).astype(o_ref.dtype)


def _ffn_in(x, g, w, w_conv, b_conv, *, seq, d_ff, tm, tn):
    m, k = x.shape
    nf = d_ff // tn
    prev_rows = 2 * ROW_TILE
    ppt = tm // prev_rows
    wconv8 = jnp.zeros((ROW_TILE, d_ff), F32).at[:FFN_CONV].set(w_conv)
    return pl.pallas_call(
        functools.partial(_ffn_in_kernel, norm_rows=min(tm, 64), tiles_per_seq=seq // tm),
        grid=(m // tm, nf),
        in_specs=[pl.BlockSpec((tm, k), lambda i, j: (i, 0)),
                  pl.BlockSpec((prev_rows, k), lambda i, j: (jnp.maximum(i * ppt - 1, 0), 0)),
                  pl.BlockSpec((1, k), lambda i, j: (0, 0)),
                  pl.BlockSpec((k, tn), lambda i, j: (0, j)),
                  pl.BlockSpec((k, tn), lambda i, j: (0, nf + j)),
                  pl.BlockSpec((ROW_TILE, tn), lambda i, j: (0, j)),
                  pl.BlockSpec((1, tn), lambda i, j: (0, j))],
        out_specs=[pl.BlockSpec((tm, tn), lambda i, j: (i, j)),
                   pl.BlockSpec((None, ROW_TILE, tn), lambda i, j: (i, 0, j))],
        out_shape=[jax.ShapeDtypeStruct((m, d_ff), BF16),
                   jax.ShapeDtypeStruct((m // tm, ROW_TILE, d_ff), F32)],
        scratch_shapes=[pltpu.VMEM((tm, k), BF16), pltpu.VMEM((prev_rows, k), BF16),
                        pltpu.VMEM((ROW_TILE + tm, tn), F32)],
        compiler_params=_cparams("parallel", "arbitrary"),
        name="ffn_in",
    )(x, x, g.reshape(1, k), w, w, wconv8, b_conv.reshape(1, d_ff))


def _ffn_act_kernel(gate_ref, prev_ref, up_ref, wconv_ref, b_ref, o_ref, buf, *, from_state):
    tt = gate_ref.shape[0]
    prev = prev_ref[...]
    if not from_state:
        prev = jnp.where(pl.program_id(1) > 0, prev, 0.0)
    buf[0:ROW_TILE, :] = prev
    buf[ROW_TILE:ROW_TILE + tt, :] = gate_ref[...]
    full = buf[...]
    y = full[ROW_TILE:] * wconv_ref[FFN_CONV - 1:FFN_CONV, :] + b_ref[...]
    for d in range(1, FFN_CONV):
        y = y + pltpu.roll(full, d, axis=0)[ROW_TILE:] * wconv_ref[FFN_CONV - 1 - d:FFN_CONV - d, :]
    o_ref[...] = (_silu(y) * up_ref[...]).astype(o_ref.dtype)


def _ffn_act(gu, w_conv, b_conv, state8, *, d_ff, tt, tf):
    b, t, _ = gu.shape
    from_state = state8 is not None
    nf = d_ff // tf
    wconv8 = jnp.zeros((ROW_TILE, d_ff), F32).at[:FFN_CONV].set(w_conv)
    tpb = tt // ROW_TILE
    if from_state:
        prev_arr = state8
        prev_spec = pl.BlockSpec((None, ROW_TILE, tf), lambda bi, ti, j: (bi, 0, j))
    else:
        prev_arr = gu
        prev_spec = pl.BlockSpec((None, ROW_TILE, tf), lambda bi, ti, j: (bi, jnp.maximum(ti * tpb - 1, 0), j))
    return pl.pallas_call(
        functools.partial(_ffn_act_kernel, from_state=from_state),
        grid=(b, t // tt, nf),
        in_specs=[pl.BlockSpec((None, tt, tf), lambda bi, ti, j: (bi, ti, j)),
                  prev_spec,
                  pl.BlockSpec((None, tt, tf), lambda bi, ti, j: (bi, ti, nf + j)),
                  pl.BlockSpec((ROW_TILE, tf), lambda bi, ti, j: (0, j)),
                  pl.BlockSpec((1, tf), lambda bi, ti, j: (0, j))],
        out_specs=pl.BlockSpec((None, tt, tf), lambda bi, ti, j: (bi, ti, j)),
        out_shape=jax.ShapeDtypeStruct((b, t, d_ff), BF16),
        scratch_shapes=[pltpu.VMEM((ROW_TILE + tt, tf), F32)],
        compiler_params=_cparams("parallel", "parallel", "parallel"),
        name="ffn_act",
    )(gu, prev_arr, gu, wconv8, b_conv.reshape(1, d_ff))


def _down_norm_kernel(a_ref, w_ref, r_ref, g_ref, o_ref, *, norm_rows):
    kk = pl.program_id(1)

    @pl.when(kk == 0)
    def _():
        o_ref[...] = r_ref[...]

    o_ref[...] += _dot(a_ref[...], w_ref[...])

    @pl.when(kk == pl.num_programs(1) - 1)
    def _():
        def body(r, carry):
            rows = pl.ds(pl.multiple_of(r * norm_rows, norm_rows), norm_rows)
            hh = o_ref[rows, :]
            o_ref[rows, :] = hh * lax.rsqrt(jnp.mean(hh * hh, axis=-1, keepdims=True) + EPS) * g_ref[...]
            return carry
        lax.fori_loop(0, o_ref.shape[0] // norm_rows, body, 0)


def _down_norm(a, w, resid, g, *, tm, tk):
    m, k = a.shape
    n = w.shape[1]
    return pl.pallas_call(
        functools.partial(_down_norm_kernel, norm_rows=min(tm, 64)),
        grid=(m // tm, k // tk),
        in_specs=[pl.BlockSpec((tm, tk), lambda i, kk: (i, kk)),
                  pl.BlockSpec((tk, n), lambda i, kk: (kk, 0)),
                  pl.BlockSpec((tm, n), lambda i, kk: (i, 0)),
                  pl.BlockSpec((1, n), lambda i, kk: (0, 0))],
        out_specs=pl.BlockSpec((tm, n), lambda i, kk: (i, 0)),
        out_shape=jax.ShapeDtypeStruct((m, n), F32),
        compiler_params=_cparams("parallel", "arbitrary"),
        name="down_norm",
    )(a, w, resid, g.reshape(1, n))


def _row_tile(rows, cap):
    tile = rows
    while tile > cap:
        assert tile % 2 == 0
        tile //= 2
    return tile


def _trunk(x, attend, gdn_state, ffn_state8, wts, *, dims, chunk, t_valid):
    b, t, d = x.shape
    m = b * t
    hb, dk, d_ff = dims["hb"], dims["dk"], dims["d_ff"]
    wa_width = dims["width_a"]
    x2 = x.reshape(m, d)
    tm = _row_tile(m, 1024)
    proj, small = _norm_matmul(x2, wts["w_norm_mix"], wts["w_in_main"], wts["w_in_small"], tm=tm, tn=1024)
    cols = proj.shape[1]
    proj3 = proj.reshape(b, t, cols)
    small3 = small.reshape(b, t, LANE)
    o_a = attend(proj3)
    conv_state8, s0 = gdn_state
    wv = hb * dk
    o_b, ssm_new = _gdn(proj3, small3, wts["w_conv_gdn"], wts["a_log"], wts["dt_bias"], wts["w_norm_gdn"],
                        conv_state8, s0, n_heads=hb, dk=dk, chunk=chunk, t_valid=t_valid,
                        qkv_block=(3 * wa_width) // (3 * wv), z_block=(3 * wa_width + 3 * wv) // wv)
    gate_a_col = 3 * wa_width + 4 * wv
    merged = _merge(o_a.reshape(m, wa_width), o_b.reshape(m, wv), proj, wts["w_branch_a"], wts["w_branch_b"],
                    gate_a_col=gate_a_col, gate_b_col=gate_a_col + d, tm=tm, tn=1024)
    h = _matmul_res(merged, wts["w_out"], x2, tm=tm, tn=1024)
    n_keep = FFN_CONV - 1
    if ffn_state8 is None:
        tm_seq = _row_tile(t, tm)
        hidden, tails = _ffn_in(h, wts["w_norm_ffn"], wts["w_ffn_in"], wts["w_ffn_conv"], wts["b_ffn_conv"],
                                seq=t, d_ff=d_ff, tm=tm_seq, tn=512)
        ffn_new = tails.reshape(b, t // tm_seq, ROW_TILE, d_ff)[:, -1, ROW_TILE - n_keep:]
    else:
        gu = _norm_matmul(h, wts["w_norm_ffn"], wts["w_ffn_in"], tm=tm, tn=1024).reshape(b, t, 2 * d_ff)
        hidden = _ffn_act(gu, wts["w_ffn_conv"], wts["b_ffn_conv"], ffn_state8, d_ff=d_ff, tt=t, tf=d_ff)
        ffn_new = gu[:, t_valid - n_keep:t_valid, :d_ff]
    y = _down_norm(hidden.reshape(m, d_ff), wts["w_ffn_down"], h, wts["w_norm_final"], tm=tm, tk=512)
    return y.reshape(b, t, d), proj3, ffn_new, ssm_new


def kernel(x_prompt, x_sample, cache_k, cache_v, state_conv, state_ssm, state_ffn_conv, page_table, w_norm_mix, w_in, lambda_q1, lambda_k1, lambda_q2, lambda_k2, w_subln, w_conv_gdn, a_log, dt_bias, w_norm_gdn, w_branch_a, w_branch_b, w_out, w_norm_ffn, w_ffn_in, w_ffn_conv, b_ffn_conv, w_ffn_down, w_norm_final):
    depth = w_in.shape[0]
    assert depth == 1, "single-layer trunk"
    l = 0
    bp, seq, d = x_prompt.shape
    bs, dec_seq, _ = x_sample.shape
    ha = cache_k.shape[3]
    width_a = ha * cache_k.shape[4]
    hb, dk = state_ssm.shape[2], state_ssm.shape[3]
    wv = hb * dk
    d_ff = w_ffn_conv.shape[2]
    n_pool, page = cache_k.shape[1], cache_k.shape[2]
    dims = dict(hb=hb, dk=dk, d_ff=d_ff, width_a=width_a)
    lam_init = 0.8 - 0.6 * math.exp(-0.3 * l)
    slopes = 2.0 ** (-8.0 * jnp.arange(1, ha + 1, dtype=F32) / ha)

    c_small = 3 * width_a + 4 * wv
    w_in_l = w_in[l]
    wts = dict(
        w_norm_mix=w_norm_mix[l],
        w_in_main=jnp.concatenate([w_in_l[:, :c_small], w_in_l[:, c_small + 2 * hb:]], axis=1).astype(BF16),
        w_in_small=jnp.pad(w_in_l[:, c_small:c_small + 2 * hb], ((0, 0), (0, LANE - 2 * hb))).astype(BF16),
        w_conv_gdn=w_conv_gdn[l], a_log=a_log[l], dt_bias=dt_bias[l], w_norm_gdn=w_norm_gdn[l],
        w_branch_a=w_branch_a[l].astype(BF16), w_branch_b=w_branch_b[l].astype(BF16),
        w_out=w_out[l].astype(BF16), w_norm_ffn=w_norm_ffn[l], w_ffn_in=w_ffn_in[l].astype(BF16),
        w_ffn_conv=w_ffn_conv[l], b_ffn_conv=b_ffn_conv[l], w_ffn_down=w_ffn_down[l].astype(BF16),
        w_norm_final=w_norm_final,
    )
    lam_params = jnp.zeros((ROW_TILE, LANE), F32)
    lam_params = (lam_params.at[0, :HEAD_DIM_A].set(lambda_q1[l]).at[1, :HEAD_DIM_A].set(lambda_k1[l])
                  .at[2, :HEAD_DIM_A].set(lambda_q2[l]).at[3, :HEAD_DIM_A].set(lambda_k2[l]))

    attend_p = functools.partial(_attn_prompt, slopes=slopes, lam_params=lam_params, w_subln=w_subln[l],
                                 n_heads=ha, lam_init=lam_init, tq=512)
    y_p, proj_p, gu_p, ssm_p = _trunk(x_prompt, attend_p, (None, None), None, wts, dims=dims,
                                      chunk=GDN_CHUNK, t_valid=GDN_CHUNK)

    assert dec_seq >= GDN_CONV - 1 and dec_seq >= FFN_CONV - 1
    pad_t = SAMPLE_ROWS - dec_seq
    x_s = jnp.pad(x_sample, ((0, 0), (0, pad_t), (0, 0)))
    conv_state8 = jnp.pad(state_conv[l], ((0, 0), (ROW_TILE - (GDN_CONV - 1), 0), (0, 0)))
    ffn_state8 = jnp.pad(state_ffn_conv[l], ((0, 0), (ROW_TILE - (FFN_CONV - 1), 0), (0, 0)))
    ck = cache_k.reshape(depth * n_pool, page, ha, width_a // ha)
    cv = cache_v.reshape(depth * n_pool, page, ha, width_a // ha)
    attend_s = functools.partial(_attn_sample, cache_k=ck, cache_v=cv, page_table=page_table + l * n_pool,
                                 slopes=slopes, lam_params=lam_params, w_subln=w_subln[l], n_heads=ha,
                                 dec_seq=dec_seq, lam_init=lam_init, n_group=16)
    y_s, proj_s, gu_s, ssm_s = _trunk(x_s, attend_s, (conv_state8, state_ssm[l]), ffn_state8, wts, dims=dims,
                                      chunk=SAMPLE_ROWS, t_valid=dec_seq)

    def outputs(y, proj, ffn_new, ssm, t_real):
        bb = y.shape[0]
        k_rows = proj[:, :t_real, width_a:2 * width_a].reshape(1, bb, t_real, ha, width_a // ha)
        v_rows = proj[:, :t_real, 2 * width_a:3 * width_a].reshape(1, bb, t_real, ha, width_a // ha)
        conv_new = proj[:, t_real - (GDN_CONV - 1):t_real, 3 * width_a:3 * width_a + 3 * wv][None]
        return y[:, :t_real], k_rows, v_rows, conv_new, ssm[None], ffn_new[None]

    yp, kp, vp, cp, sp, fp = outputs(y_p, proj_p, gu_p, ssm_p, seq)
    ys, ks, vs, cs, ss, fs = outputs(y_s, proj_s, gu_s, ssm_s, dec_seq)
    return (yp, ys, kp, vp, cp, sp, fp, ks, vs, cs, ss, fs)
```

```python
import functools
import math

import jax
import jax.numpy as jnp
from jax import lax
from jax.experimental import pallas as pl
from jax.experimental.pallas import tpu as pltpu

F32 = jnp.float32
BF16 = jnp.bfloat16
EPS = 1e-6
NEG = -1e30
LOG2E = math.log2(math.e)
ROW_TILE = 8
SAMPLE_ROWS = 16
LANE = 128
VMEM_LIMIT = 60 * 1024 * 1024

HEAD_DIM_A = 64
GDN_CONV = 4
GDN_CHUNK = 64
FFN_CONV = 3


def _cparams(*sem):
    return pltpu.CompilerParams(dimension_semantics=sem, vmem_limit_bytes=VMEM_LIMIT)


def _dot(a, b):
    return jnp.dot(a, b, preferred_element_type=F32)


def _dot_nt(a, b):
    return lax.dot_general(a, b, (((1,), (1,)), ((), ())), preferred_element_type=F32)


def _dot_tn(a, b):
    return lax.dot_general(a, b, (((0,), (0,)), ((), ())), preferred_element_type=F32)


def _sigmoid(x):
    return 0.5 * jnp.tanh(0.5 * x) + 0.5


def _silu(x):
    return x * _sigmoid(x)


def _norm_matmul_kernel(*refs, has_small, norm_rows, head_tiles):
    if has_small:
        x_ref, g_ref, w_ref, w2_ref, o_ref, o2_ref, *head_refs, xn_ref = refs
    else:
        x_ref, g_ref, w_ref, o_ref, xn_ref = refs

    @pl.when(pl.program_id(1) == 0)
    def _():
        def body(r, carry):
            rows = pl.ds(pl.multiple_of(r * norm_rows, norm_rows), norm_rows)
            x = x_ref[rows, :]
            xn = x * lax.rsqrt(jnp.mean(x * x, axis=-1, keepdims=True) + EPS) * g_ref[...]
            xn_ref[rows, :] = xn.astype(BF16)
            return carry
        lax.fori_loop(0, x_ref.shape[0] // norm_rows, body, 0)
        if has_small:
            o2_ref[...] = _dot(xn_ref[...], w2_ref[...])

    res = _dot(xn_ref[...], w_ref[...])
    o_ref[...] = res
    for tile, ref in zip(head_tiles, head_refs if has_small else ()):
        @pl.when(pl.program_id(1) == tile)
        def _(ref=ref):
            ref[...] = res.reshape(ref.shape)


def _norm_matmul(x, g, w, w2=None, *, tm, tn, head_tiles=()):
    m, k = x.shape
    n = w.shape[1]
    has_small = w2 is not None
    in_specs = [pl.BlockSpec((tm, k), lambda i, j: (i, 0)),
                pl.BlockSpec((1, k), lambda i, j: (0, 0)),
                pl.BlockSpec((k, tn), lambda i, j: (0, j))]
    out_specs = [pl.BlockSpec((tm, tn), lambda i, j: (i, j))]
    out_shape = [jax.ShapeDtypeStruct((m, n), F32)]
    args = [x, g.reshape(1, k), w]
    if has_small:
        in_specs.append(pl.BlockSpec((k, LANE), lambda i, j: (0, 0)))
        out_specs.append(pl.BlockSpec((tm, LANE), lambda i, j: (i, 0)))
        out_shape.append(jax.ShapeDtypeStruct((m, LANE), F32))
        args.append(w2)
        for _ in head_tiles:
            out_specs.append(pl.BlockSpec((tm, tn // LANE, LANE), lambda i, j: (i, 0, 0)))
            out_shape.append(jax.ShapeDtypeStruct((m, tn // LANE, LANE), F32))
    outs = pl.pallas_call(
        functools.partial(_norm_matmul_kernel, has_small=has_small, norm_rows=min(tm, 64),
                          head_tiles=tuple(head_tiles)),
        grid=(m // tm, n // tn),
        in_specs=in_specs, out_specs=out_specs, out_shape=out_shape,
        scratch_shapes=[pltpu.VMEM((tm, k), BF16)],
        compiler_params=_cparams("parallel", "arbitrary"),
        name="norm_matmul",
    )(*args)
    return outs if has_small else outs[0]


def _lambda_value(lam_ref, lam_init):
    p = lam_ref[...]
    t1 = jnp.sum(p[0:1, :] * p[1:2, :], axis=-1, keepdims=True)
    t2 = jnp.sum(p[2:3, :] * p[3:4, :], axis=-1, keepdims=True)
    return jnp.exp(t1) - jnp.exp(t2) + lam_init


def _subln(o, w, lam_init):
    n = o * lax.rsqrt(jnp.mean(o * o, axis=-1, keepdims=True) + EPS) * w
    return n * (1.0 - lam_init)


def _attn_prompt_kernel(slope_ref, q_ref, k_ref, v_ref, rel_ref, lam_ref, wsub_ref, o_ref,
                        kb_sc, vt_sc, bias_sc, q2_sc, *, tq, n_q, lam_init):
    h = pl.program_id(1)
    qi = pl.program_id(2)

    slope2 = slope_ref[h] * LOG2E

    @pl.when(qi == 0)
    def _():
        kb_sc[...] = k_ref[...].astype(BF16)
        vt_sc[...] = v_ref[...].T.astype(BF16)
        bias_sc[...] = slope2 * rel_ref[...]

    q = q_ref[...] * (HEAD_DIM_A ** -0.5 * LOG2E)
    lane = lax.broadcasted_iota(jnp.int32, (1, LANE), 1)
    q2_sc[0:tq, :] = jnp.where(lane < HEAD_DIM_A, q, 0.0).astype(BF16)
    q2_sc[tq:2 * tq, :] = jnp.where(lane >= HEAD_DIM_A, q, 0.0).astype(BF16)

    def scores(ki):
        return _dot_nt(kb_sc[ki * tq:(ki + 1) * tq, :], q2_sc[...])

    def update(state, s, ki, n_before):
        m_prev, l_prev, acc = state
        bias = bias_sc[...]
        if ki == n_before:
            bias = jnp.where(rel_ref[...] <= 0.0, bias, NEG)
        s = s + bias
        shift = slope2 * float((ki - n_before) * tq)
        m_new = jnp.maximum(m_prev, jnp.max(s, axis=0, keepdims=True) + shift)
        alpha = jnp.exp2(m_prev - m_new)
        p = jnp.exp2(s - (m_new - shift))
        l_new = alpha * l_prev + jnp.sum(p, axis=0, keepdims=True)
        acc = alpha * acc + _dot(vt_sc[:, ki * tq:(ki + 1) * tq], p.astype(BF16))
        return m_new, l_new, acc

    for n_before in range(n_q):
        @pl.when(qi == n_before)
        def _(n_before=n_before):
            state = (jnp.full((1, 2 * tq), NEG, F32), jnp.zeros((1, 2 * tq), F32),
                     jnp.zeros((LANE, 2 * tq), F32))
            s_cur = scores(0)
            for ki in range(n_before):
                s_next = scores(ki + 1)
                state = update(state, s_cur, ki, n_before)
                s_cur = s_next
            _, l_fin, acc = update(state, s_cur, n_before, n_before)
            lam = _lambda_value(lam_ref, lam_init)
            n = acc / l_fin
            o = (n[:, 0:tq] - lam * n[:, tq:2 * tq]).T
            o_ref[...] = _subln(o, wsub_ref[...], lam_init).astype(o_ref.dtype)


def _attn_prompt(proj, slopes, lam_params, w_subln, *, n_heads, lam_init, tq):
    b, t, _ = proj.shape
    rel = (jnp.arange(tq, dtype=jnp.int32)[:, None] - jnp.arange(tq, dtype=jnp.int32)[None, :]).astype(F32)
    rel = jnp.concatenate([rel, rel], axis=1)
    hh = n_heads
    grid_spec = pltpu.PrefetchScalarGridSpec(
        num_scalar_prefetch=0,
        grid=(b, hh, t // tq),
        in_specs=[
            pl.BlockSpec(memory_space=pltpu.SMEM),
            pl.BlockSpec((None, tq, LANE), lambda bi, h, qi: (bi, qi, h)),
            pl.BlockSpec((None, t, LANE), lambda bi, h, qi: (bi, 0, hh + h)),
            pl.BlockSpec((None, t, LANE), lambda bi, h, qi: (bi, 0, 2 * hh + h)),
            pl.BlockSpec((tq, 2 * tq), lambda bi, h, qi: (0, 0)),
            pl.BlockSpec((ROW_TILE, LANE), lambda bi, h, qi: (0, 0)),
            pl.BlockSpec((1, LANE), lambda bi, h, qi: (0, 0)),
        ],
        out_specs=pl.BlockSpec((None, tq, LANE), lambda bi, h, qi: (bi, qi, h)),
        scratch_shapes=[pltpu.VMEM((t, LANE), BF16), pltpu.VMEM((LANE, t), BF16),
                        pltpu.VMEM((tq, 2 * tq), F32), pltpu.VMEM((2 * tq, LANE), BF16)],
    )
    return pl.pallas_call(
        functools.partial(_attn_prompt_kernel, tq=tq, n_q=t // tq, lam_init=lam_init),
        grid_spec=grid_spec,
        out_shape=jax.ShapeDtypeStruct((b, t, n_heads * LANE), BF16),
        compiler_params=_cparams("parallel", "parallel", "arbitrary"),
        name="attn_prompt",
    )(slopes, proj, proj, proj, rel, lam_params, w_subln.reshape(1, LANE))


def _attn_sample_kernel(pt_ref, q_ref, kn_ref, vn_ref, slope_ref, trow_ref, lam_ref, wsub_ref, *rest,
                        n_group, n_heads, dec_seq, past_len, page, lam_init):
    k_refs = rest[:n_group]
    v_refs = rest[n_group:2 * n_group]
    o_ref, qm_sc, bias_sc, m_sc, l_sc, acc_sc = rest[2 * n_group:]
    p_id = pl.program_id(1)
    hs = n_heads * ROW_TILE
    hshift = int(math.log2(n_heads))
    slope2 = slope_ref[:, 0:1] * LOG2E
    trow = trow_ref[:, 0:1]

    def col_info(n_cols):
        row = lax.broadcasted_iota(jnp.int32, (hs, n_cols), 0)
        col = lax.broadcasted_iota(jnp.int32, (hs, n_cols), 1)
        same_head = (col & (n_heads - 1)) == (row >> int(math.log2(ROW_TILE)))
        return same_head, (col >> hshift).astype(F32)

    @pl.when(p_id == 0)
    def _():
        q = q_ref[...] * (HEAD_DIM_A ** -0.5 * LOG2E)
        row = lax.broadcasted_iota(jnp.int32, (hs, LANE), 0)
        lane = lax.broadcasted_iota(jnp.int32, (hs, LANE), 1)
        second_map = ((row >> int(math.log2(dec_seq))) & 1) == 1
        qm = jnp.where((lane >= HEAD_DIM_A) == second_map, q, 0.0).astype(BF16)
        qm_sc[...] = qm
        same_head, key = col_info(page * n_heads)
        bias_sc[...] = jnp.where(same_head, slope2 * key, NEG)
        same_head, key = col_info(ROW_TILE * n_heads)
        dist = trow - key
        s = _dot_nt(qm, kn_ref[...].astype(BF16)) - slope2 * dist
        s = jnp.where(same_head & (dist >= 0.0) & (key < dec_seq), s, NEG)
        m0 = jnp.max(s, axis=-1, keepdims=True)
        p = jnp.exp2(s - m0)
        m_sc[...] = m0
        l_sc[...] = jnp.sum(p, axis=-1, keepdims=True)
        acc_sc[...] = _dot(p.astype(BF16), vn_ref[...].astype(BF16))

    qm = qm_sc[...]
    bias = bias_sc[...]
    scores, shifts = [], []
    for g in range(n_group):
        kg = k_refs[g][...].reshape(page * n_heads, LANE).astype(BF16)
        scores.append(_dot_nt(qm, kg) + bias)
        kpos0 = ((p_id * n_group + g) * page).astype(F32)
        shifts.append(slope2 * ((past_len + trow) - kpos0))
    m_prev = m_sc[...]
    m_new = m_prev
    for s, sh in zip(scores, shifts):
        m_new = jnp.maximum(m_new, jnp.max(s, axis=-1, keepdims=True) - sh)
    alpha = jnp.exp2(m_prev - m_new)
    l_new = alpha * l_sc[...]
    acc = alpha * acc_sc[...]
    for g in range(n_group):
        p = jnp.exp2(scores[g] - (m_new + shifts[g]))
        l_new = l_new + jnp.sum(p, axis=-1, keepdims=True)
        vg = v_refs[g][...].reshape(page * n_heads, LANE).astype(BF16)
        acc = acc + _dot(p.astype(BF16), vg)
    m_sc[...] = m_new
    l_sc[...] = l_new
    acc_sc[...] = acc

    @pl.when(p_id == pl.num_programs(1) - 1)
    def _():
        lam = _lambda_value(lam_ref, lam_init)
        n = acc_sc[...] / l_sc[...]
        o = n - lam * pltpu.roll(n, hs - dec_seq, axis=0)
        o = _subln(o, wsub_ref[...], lam_init)
        pad = jnp.zeros((o_ref.shape[0] - ROW_TILE, LANE), F32)
        for hd in range(n_heads):
            rows = jnp.concatenate([o[hd * ROW_TILE:(hd + 1) * ROW_TILE], pad], axis=0)
            o_ref[:, hd * LANE:(hd + 1) * LANE] = rows.astype(o_ref.dtype)


def _attn_sample(proj, cache_k, cache_v, page_table, slopes, lam_params, w_subln, *,
                 n_heads, dec_seq, lam_init, n_group):
    assert 2 * dec_seq == ROW_TILE
    b, t_pad, _ = proj.shape
    n_pages = page_table.shape[1]
    page = cache_k.shape[1]
    width = n_heads * LANE
    hs = n_heads * ROW_TILE
    past_len = n_pages * page
    q4 = proj[:, :dec_seq, :width].reshape(b, dec_seq, n_heads, LANE).transpose(0, 2, 1, 3)
    q_rows = jnp.concatenate([q4, q4], axis=2).reshape(b, hs, LANE)
    kn_rows = proj[:, :ROW_TILE, width:2 * width].reshape(b, ROW_TILE * n_heads, LANE)
    vn_rows = proj[:, :ROW_TILE, 2 * width:3 * width].reshape(b, ROW_TILE * n_heads, LANE)
    rows = jnp.arange(hs)
    slope_rows = jnp.broadcast_to(slopes[rows // ROW_TILE][:, None], (hs, LANE)).astype(F32)
    trow = jnp.broadcast_to((rows % dec_seq).astype(F32)[:, None], (hs, LANE))
    const = lambda bi, p, pt: (0, 0)

    def page_spec(g):
        return pl.BlockSpec((None, page, n_heads, LANE), lambda bi, p, pt: (pt[bi, p * n_group + g], 0, 0, 0))

    grid_spec = pltpu.PrefetchScalarGridSpec(
        num_scalar_prefetch=1,
        grid=(b, n_pages // n_group),
        in_specs=[
            pl.BlockSpec((None, hs, LANE), lambda bi, p, pt: (bi, 0, 0)),
            pl.BlockSpec((None, ROW_TILE * n_heads, LANE), lambda bi, p, pt: (bi, 0, 0)),
            pl.BlockSpec((None, ROW_TILE * n_heads, LANE), lambda bi, p, pt: (bi, 0, 0)),
            pl.BlockSpec((hs, LANE), const),
            pl.BlockSpec((hs, LANE), const),
            pl.BlockSpec((ROW_TILE, LANE), const),
            pl.BlockSpec((1, LANE), const),
        ] + [page_spec(g) for g in range(n_group)] + [page_spec(g) for g in range(n_group)],
        out_specs=pl.BlockSpec((None, t_pad, width), lambda bi, p, pt: (bi, 0, 0)),
        scratch_shapes=[pltpu.VMEM((hs, LANE), BF16), pltpu.VMEM((hs, page * n_heads), F32),
                        pltpu.VMEM((hs, 1), F32), pltpu.VMEM((hs, 1), F32), pltpu.VMEM((hs, LANE), F32)],
    )
    return pl.pallas_call(
        functools.partial(_attn_sample_kernel, n_group=n_group, n_heads=n_heads, dec_seq=dec_seq,
                          past_len=float(past_len), page=page, lam_init=lam_init),
        grid_spec=grid_spec,
        out_shape=jax.ShapeDtypeStruct((b, t_pad, width), BF16),
        compiler_params=_cparams("parallel", "arbitrary"),
        name="attn_sample",
    )(page_table, q_rows, kn_rows, vn_rows, slope_rows, trow, lam_params, w_subln.reshape(1, LANE),
      *([cache_k] * n_group), *([cache_v] * n_group))


def _split3_dot(a_bf16, x):
    x1 = x.astype(BF16)
    r1 = x - x1.astype(F32)
    x2 = r1.astype(BF16)
    x3 = (r1 - x2.astype(F32)).astype(BF16)
    return _dot(a_bf16, x1) + _dot(a_bf16, x2) + _dot(a_bf16, x3)


def _gdn_kernel(*refs, chunk, t_valid, n_heads, dk, has_state):
    if has_state:
        (x_ref, z_ref, ba_ref, wconv_ref, prm_ref, wnorm_ref, cstate_ref, s0_ref,
         o_ref, sout_ref, xbuf, s_sc) = refs
    else:
        (x_ref, z_ref, ba_ref, wconv_ref, prm_ref, wnorm_ref,
         o_ref, sout_ref, xbuf, s_sc) = refs
    n = pl.program_id(1)
    c = chunk
    wqk = n_heads * dk

    @pl.when(n == 0)
    def _():
        if has_state:
            xbuf[0:ROW_TILE, :] = cstate_ref[...]
            s_sc[...] = s0_ref[...]
        else:
            xbuf[0:ROW_TILE, :] = jnp.zeros((ROW_TILE, xbuf.shape[1]), F32)
            s_sc[...] = jnp.zeros(s_sc.shape, F32)

    rows = x_ref.shape[0]
    n_sub = rows // c
    xbuf[ROW_TILE:ROW_TILE + rows, :] = x_ref[...]
    xfull = xbuf[...]
    y = xfull[ROW_TILE:] * wconv_ref[GDN_CONV - 1:GDN_CONV, :]
    for d in range(1, GDN_CONV):
        y = y + pltpu.roll(xfull, d, axis=0)[ROW_TILE:] * wconv_ref[GDN_CONV - 1 - d:GDN_CONV - d, :]
    if rows > ROW_TILE:
        xbuf[0:ROW_TILE, :] = xfull[rows:rows + ROW_TILE]
    act = _silu(y)

    ba = ba_ref[...]
    beta_all = _sigmoid(ba)
    sp_in = ba + prm_ref[1:2, :]
    softplus = jnp.maximum(sp_in, 0.0) + jnp.log(1.0 + jnp.exp(-jnp.abs(sp_in)))
    g_all = -jnp.exp(prm_ref[0:1, :]) * softplus
    if t_valid < c:
        rv = lax.broadcasted_iota(jnp.int32, (rows, 1), 0) < t_valid
        act = jnp.where(rv, act, 0.0)
        beta_all = jnp.where(rv, beta_all, 0.0)
        g_all = jnp.where(rv, g_all, 0.0)

    ri = lax.broadcasted_iota(jnp.int32, (c, c), 0)
    ci = lax.broadcasted_iota(jnp.int32, (c, c), 1)
    strict = ri > ci
    incl = ri >= ci
    eye = (ri == ci).astype(F32)
    tri = incl.astype(BF16)

    units = [(sub, h) for sub in range(n_sub) for h in range(n_heads)]
    n_double = int(math.log2(c)) - 1
    gc_sub = [_split3_dot(tri, g_all[sub * c:(sub + 1) * c]) for sub in range(n_sub)]
    qs, ks, vs = [], [], []
    for sub, h in units:
        r0 = sub * c
        q = act[r0:r0 + c, h * dk:(h + 1) * dk]
        k = act[r0:r0 + c, wqk + h * dk:wqk + (h + 1) * dk]
        qs.append(q * lax.rsqrt(jnp.sum(q * q, axis=-1, keepdims=True) + EPS) * (dk ** -0.5))
        ks.append(k * lax.rsqrt(jnp.sum(k * k, axis=-1, keepdims=True) + EPS))
        vs.append(act[r0:r0 + c, 2 * wqk + h * dk:2 * wqk + (h + 1) * dk])
    betas = [beta_all[sub * c:(sub + 1) * c, h:h + 1] for sub, h in units]
    gcs = [gc_sub[sub][:, n_heads + h:n_heads + h + 1] for sub, h in units]
    decays = [jnp.exp(jnp.where(incl, gc - jnp.sum(eye * gc, axis=0, keepdims=True), NEG)) for gc in gcs]
    kbs = [k * b for k, b in zip(ks, betas)]
    kbf = [k.astype(BF16) for k in ks]
    kq = [_dot_nt(jnp.concatenate([kb.astype(BF16), q.astype(BF16)], axis=0), kf)
          for kb, q, kf in zip(kbs, qs, kbf)]
    xps = [jnp.where(strict, -(m[:c] * d), 0.0) for m, d in zip(kq, decays)]
    tinvs = [eye + x for x in xps]
    xps = [_dot(x.astype(BF16), x.astype(BF16)) for x in xps]
    for i in range(n_double):
        xpb = [x.astype(BF16) for x in xps]
        tinvs = [t + _dot(xb, t.astype(BF16)) for t, xb in zip(tinvs, xpb)]
        if i < n_double - 1:
            xps = [_dot(xb, xb) for xb in xpb]
    egcs = [jnp.exp(gc) for gc in gcs]
    uw = [_dot(t.astype(BF16), jnp.concatenate([(v * b).astype(BF16), (kb * e).astype(BF16)], axis=1))
          for t, v, b, kb, e in zip(tinvs, vs, betas, kbs, egcs)]
    g_lasts = [gc[c - 1:c, :] for gc in gcs]
    k_decs = [(k * jnp.exp(gl - gc)).astype(BF16) for k, gl, gc in zip(ks, g_lasts, gcs)]
    ws_lhs = [jnp.concatenate([m[:, dk:].astype(BF16), (q * e).astype(BF16)], axis=0)
              for m, q, e in zip(uw, qs, egcs)]
    qk_dec = [(m[c:] * d).astype(BF16) for m, d in zip(kq, decays)]
    states = [s_sc[h] for h in range(n_heads)]
    for sub in range(n_sub):
        ids = [sub * n_heads + h for h in range(n_heads)]
        ws_qs = [_dot(ws_lhs[i], s.astype(BF16)) for i, s in zip(ids, states)]
        v_newb = [(uw[i][:, :dk] - x[:c]).astype(BF16) for i, x in zip(ids, ws_qs)]
        outs = [x[c:] + _dot(qk_dec[i], vn) for i, x, vn in zip(ids, ws_qs, v_newb)]
        states = [s * jnp.exp(g_lasts[i]) + _dot_tn(k_decs[i], vn) for i, s, vn in zip(ids, states, v_newb)]
        for h, o in enumerate(outs):
            on = o * lax.rsqrt(jnp.mean(o * o, axis=-1, keepdims=True) + EPS) * wnorm_ref[...]
            zz = z_ref[sub * c:(sub + 1) * c, h * dk:(h + 1) * dk]
            o_ref[sub * c:(sub + 1) * c, h * dk:(h + 1) * dk] = (on * _silu(zz)).astype(o_ref.dtype)
    for h in range(n_heads):
        s_sc[h] = states[h]

    @pl.when(n == pl.num_programs(1) - 1)
    def _():
        sout_ref[...] = s_sc[...]


def _gdn(proj, small, w_conv, a_log, dt_bias, w_norm, conv_state8, s0, *, n_heads, dk, chunk, n_sub, t_valid,
         qkv_block, z_block):
    b, t, _ = proj.shape
    wv = n_heads * dk
    cch = 3 * wv
    has_state = s0 is not None
    wconv8 = jnp.zeros((ROW_TILE, cch), F32).at[:GDN_CONV].set(w_conv)
    prm = jnp.zeros((ROW_TILE, LANE), F32)
    prm = prm.at[0, n_heads:2 * n_heads].set(a_log).at[1, n_heads:2 * n_heads].set(dt_bias)
    const = lambda bi, n: (0, 0)
    assert n_sub == 1 or t_valid == chunk
    rows = n_sub * chunk
    in_specs = [
        pl.BlockSpec((None, rows, cch), lambda bi, n: (bi, n, qkv_block)),
        pl.BlockSpec((None, rows, wv), lambda bi, n: (bi, n, z_block)),
        pl.BlockSpec((None, rows, LANE), lambda bi, n: (bi, n, 0)),
        pl.BlockSpec((ROW_TILE, cch), const),
        pl.BlockSpec((ROW_TILE, LANE), const),
        pl.BlockSpec((1, dk), const),
    ]
    args = [proj, proj, small, wconv8, prm, w_norm.reshape(1, dk)]
    if has_state:
        in_specs += [pl.BlockSpec((None, ROW_TILE, cch), lambda bi, n: (bi, 0, 0)),
                     pl.BlockSpec((None, n_heads, dk, dk), lambda bi, n: (bi, 0, 0, 0))]
        args += [conv_state8, s0]
    o, s_out = pl.pallas_call(
        functools.partial(_gdn_kernel, chunk=chunk, t_valid=t_valid, n_heads=n_heads, dk=dk,
                          has_state=has_state),
        grid=(b, t // rows),
        in_specs=in_specs,
        out_specs=[pl.BlockSpec((None, rows, wv), lambda bi, n: (bi, n, 0)),
                   pl.BlockSpec((None, n_heads, dk, dk), lambda bi, n: (bi, 0, 0, 0))],
        out_shape=[jax.ShapeDtypeStruct((b, t, wv), BF16),
                   jax.ShapeDtypeStruct((b, n_heads, dk, dk), F32)],
        scratch_shapes=[pltpu.VMEM((ROW_TILE + rows, cch), F32), pltpu.VMEM((n_heads, dk, dk), F32)],
        compiler_params=_cparams("parallel", "arbitrary"),
        name="gdn",
    )(*args)
    return o, s_out


def _merge_kernel(oa_ref, ob_ref, ga_ref, gb_ref, wa_ref, wb_ref, o_ref):
    pa = _dot(oa_ref[...], wa_ref[...])
    pb = _dot(ob_ref[...], wb_ref[...])
    o_ref[...] = (_sigmoid(ga_ref[...]) * pa + _sigmoid(gb_ref[...]) * pb).astype(o_ref.dtype)


def _merge(o_a, o_b, proj, w_a, w_b, *, gate_a_col, gate_b_col, tm, tn):
    m, ka = o_a.shape
    n = w_a.shape[1]
    return pl.pallas_call(
        _merge_kernel,
        grid=(m // tm, n // tn),
        in_specs=[pl.BlockSpec((tm, ka), lambda i, j: (i, 0)),
                  pl.BlockSpec((tm, o_b.shape[1]), lambda i, j: (i, 0)),
                  pl.BlockSpec((tm, tn), lambda i, j: (i, gate_a_col // tn + j)),
                  pl.BlockSpec((tm, tn), lambda i, j: (i, gate_b_col // tn + j)),
                  pl.BlockSpec((ka, tn), lambda i, j: (0, j)),
                  pl.BlockSpec((w_b.shape[0], tn), lambda i, j: (0, j))],
        out_specs=pl.BlockSpec((tm, tn), lambda i, j: (i, j)),
        out_shape=jax.ShapeDtypeStruct((m, n), BF16),
        compiler_params=_cparams("parallel", "arbitrary"),
        name="merge",
    )(o_a, o_b, proj, proj, w_a, w_b)


def _matmul_res_kernel(a_ref, w_ref, r_ref, o_ref):
    o_ref[...] = r_ref[...] + _dot(a_ref[...], w_ref[...])


def _matmul_res(a, w, resid, *, tm, tn):
    m, k = a.shape
    n = w.shape[1]
    return pl.pallas_call(
        _matmul_res_kernel,
        grid=(m // tm, n // tn),
        in_specs=[pl.BlockSpec((tm, k), lambda i, j: (i, 0)),
                  pl.BlockSpec((k, tn), lambda i, j: (0, j)),
                  pl.BlockSpec((tm, tn), lambda i, j: (i, j))],
        out_specs=pl.BlockSpec((tm, tn), lambda i, j: (i, j)),
        out_shape=jax.ShapeDtypeStruct((m, n), F32),
        compiler_params=_cparams("parallel", "arbitrary"),
        name="matmul_res",
    )(a, w, resid)


def _rmsnorm_rows(x, g):
    return (x * lax.rsqrt(jnp.mean(x * x, axis=-1, keepdims=True) + EPS) * g).astype(BF16)


def _ffn_in_kernel(x_ref, xprev_ref, g_ref, wg_ref, wu_ref, wconv_ref, b_ref, o_ref, tail_ref,
                   xn_sc, xp_sc, buf, *, norm_rows, tiles_per_seq):
    tm = x_ref.shape[0]

    @pl.when(pl.program_id(1) == 0)
    def _():
        def body(r, carry):
            rows = pl.ds(pl.multiple_of(r * norm_rows, norm_rows), norm_rows)
            xn_sc[rows, :] = _rmsnorm_rows(x_ref[rows, :], g_ref[...])
            return carry
        lax.fori_loop(0, tm // norm_rows, body, 0)
        xp_sc[...] = _rmsnorm_rows(xprev_ref[...], g_ref[...])

    gate = _dot(xn_sc[...], wg_ref[...])
    up = _dot(xn_sc[...], wu_ref[...])
    prev = _dot(xp_sc[...], wg_ref[...])[xprev_ref.shape[0] - ROW_TILE:]
    prev = jnp.where(pl.program_id(0) % tiles_per_seq > 0, prev, 0.0)
    tail_ref[...] = gate[tm - ROW_TILE:]
    buf[0:ROW_TILE, :] = prev
    buf[ROW_TILE:ROW_TILE + tm, :] = gate
    full = buf[...]
    y = full[ROW_TILE:] * wconv_ref[FFN_CONV - 1:FFN_CONV, :] + b_ref[...]
    for d in range(1, FFN_CONV):
        y = y + pltpu.roll(full, d, axis=0)[ROW_TILE:] * wconv_ref[FFN_CONV - 1 - d:FFN_CONV - d, :]
    o_ref[...] = (_silu(y) * up).astype(o_ref.dtype)


def _ffn_in(x, g, w, w_conv, b_conv, *, seq, d_ff, tm, tn):
    m, k = x.shape
    nf = d_ff // tn
    prev_rows = 2 * ROW_TILE
    ppt = tm // prev_rows
    wconv8 = jnp.zeros((ROW_TILE, d_ff), F32).at[:FFN_CONV].set(w_conv)
    return pl.pallas_call(
        functools.partial(_ffn_in_kernel, norm_rows=min(tm, 64), tiles_per_seq=seq // tm),
        grid=(m // tm, nf),
        in_specs=[pl.BlockSpec((tm, k), lambda i, j: (i, 0)),
                  pl.BlockSpec((prev_rows, k), lambda i, j: (jnp.maximum(i * ppt - 1, 0), 0)),
                  pl.BlockSpec((1, k), lambda i, j: (0, 0)),
                  pl.BlockSpec((k, tn), lambda i, j: (0, j)),
                  pl.BlockSpec((k, tn), lambda i, j: (0, nf + j)),
                  pl.BlockSpec((ROW_TILE, tn), lambda i, j: (0, j)),
                  pl.BlockSpec((1, tn), lambda i, j: (0, j))],
        out_specs=[pl.BlockSpec((tm, tn), lambda i, j: (i, j)),
                   pl.BlockSpec((None, ROW_TILE, tn), lambda i, j: (i, 0, j))],
        out_shape=[jax.ShapeDtypeStruct((m, d_ff), BF16),
                   jax.ShapeDtypeStruct((m // tm, ROW_TILE, d_ff), F32)],
        scratch_shapes=[pltpu.VMEM((tm, k), BF16), pltpu.VMEM((prev_rows, k), BF16),
                        pltpu.VMEM((ROW_TILE + tm, tn), F32)],
        compiler_params=_cparams("parallel", "arbitrary"),
        name="ffn_in",
    )(x, x, g.reshape(1, k), w, w, wconv8, b_conv.reshape(1, d_ff))


def _ffn_act_kernel(gate_ref, prev_ref, up_ref, wconv_ref, b_ref, o_ref, buf, *, from_state):
    tt = gate_ref.shape[0]
    prev = prev_ref[...]
    if not from_state:
        prev = jnp.where(pl.program_id(1) > 0, prev, 0.0)
    buf[0:ROW_TILE, :] = prev
    buf[ROW_TILE:ROW_TILE + tt, :] = gate_ref[...]
    full = buf[...]
    y = full[ROW_TILE:] * wconv_ref[FFN_CONV - 1:FFN_CONV, :] + b_ref[...]
    for d in range(1, FFN_CONV):
        y = y + pltpu.roll(full, d, axis=0)[ROW_TILE:] * wconv_ref[FFN_CONV - 1 - d:FFN_CONV - d, :]
    o_ref[...] = (_silu(y) * up_ref[...]).astype(o_ref.dtype)


def _ffn_act(gu, w_conv, b_conv, state8, *, d_ff, tt, tf):
    b, t, _ = gu.shape
    from_state = state8 is not None
    nf = d_ff // tf
    wconv8 = jnp.zeros((ROW_TILE, d_ff), F32).at[:FFN_CONV].set(w_conv)
    tpb = tt // ROW_TILE
    if from_state:
        prev_arr = state8
        prev_spec = pl.BlockSpec((None, ROW_TILE, tf), lambda bi, ti, j: (bi, 0, j))
    else:
        prev_arr = gu
        prev_spec = pl.BlockSpec((None, ROW_TILE, tf), lambda bi, ti, j: (bi, jnp.maximum(ti * tpb - 1, 0), j))
    return pl.pallas_call(
        functools.partial(_ffn_act_kernel, from_state=from_state),
        grid=(b, t // tt, nf),
        in_specs=[pl.BlockSpec((None, tt, tf), lambda bi, ti, j: (bi, ti, j)),
                  prev_spec,
                  pl.BlockSpec((None, tt, tf), lambda bi, ti, j: (bi, ti, nf + j)),
                  pl.BlockSpec((ROW_TILE, tf), lambda bi, ti, j: (0, j)),
                  pl.BlockSpec((1, tf), lambda bi, ti, j: (0, j))],
        out_specs=pl.BlockSpec((None, tt, tf), lambda bi, ti, j: (bi, ti, j)),
        out_shape=jax.ShapeDtypeStruct((b, t, d_ff), BF16),
        scratch_shapes=[pltpu.VMEM((ROW_TILE + tt, tf), F32)],
        compiler_params=_cparams("parallel", "parallel", "parallel"),
        name="ffn_act",
    )(gu, prev_arr, gu, wconv8, b_conv.reshape(1, d_ff))


def _down_norm_kernel(a_ref, w_ref, r_ref, g_ref, o_ref, *, norm_rows):
    kk = pl.program_id(1)

    @pl.when(kk == 0)
    def _():
        o_ref[...] = r_ref[...]

    o_ref[...] += _dot(a_ref[...], w_ref[...])

    @pl.when(kk == pl.num_programs(1) - 1)
    def _():
        def body(r, carry):
            rows = pl.ds(pl.multiple_of(r * norm_rows, norm_rows), norm_rows)
            hh = o_ref[rows, :]
            o_ref[rows, :] = hh * lax.rsqrt(jnp.mean(hh * hh, axis=-1, keepdims=True) + EPS) * g_ref[...]
            return carry
        lax.fori_loop(0, o_ref.shape[0] // norm_rows, body, 0)


def _down_norm(a, w, resid, g, *, tm, tk):
    m, k = a.shape
    n = w.shape[1]
    return pl.pallas_call(
        functools.partial(_down_norm_kernel, norm_rows=min(tm, 64)),
        grid=(m // tm, k // tk),
        in_specs=[pl.BlockSpec((tm, tk), lambda i, kk: (i, kk)),
                  pl.BlockSpec((tk, n), lambda i, kk: (kk, 0)),
                  pl.BlockSpec((tm, n), lambda i, kk: (i, 0)),
                  pl.BlockSpec((1, n), lambda i, kk: (0, 0))],
        out_specs=pl.BlockSpec((tm, n), lambda i, kk: (i, 0)),
        out_shape=jax.ShapeDtypeStruct((m, n), F32),
        compiler_params=_cparams("parallel", "arbitrary"),
        name="down_norm",
    )(a, w, resid, g.reshape(1, n))


def _row_tile(rows, cap):
    tile = rows
    while tile > cap:
        assert tile % 2 == 0
        tile //= 2
    return tile


def _trunk(x, attend, gdn_state, ffn_state8, wts, *, dims, chunk, t_valid):
    b, t, d = x.shape
    m = b * t
    hb, dk, d_ff = dims["hb"], dims["dk"], dims["d_ff"]
    wa_width = dims["width_a"]
    x2 = x.reshape(m, d)
    tm = _row_tile(m, 1024)
    tn_in = wa_width
    proj, small, k_heads, v_heads = _norm_matmul(x2, wts["w_norm_mix"], wts["w_in_main"], wts["w_in_small"],
                                                 tm=tm, tn=tn_in, head_tiles=(1, 2))
    kv_heads = tuple(a.reshape(b, t, wa_width // LANE, LANE) for a in (k_heads, v_heads))
    cols = proj.shape[1]
    proj3 = proj.reshape(b, t, cols)
    small3 = small.reshape(b, t, LANE)
    o_a = attend(proj3)
    conv_state8, s0 = gdn_state
    wv = hb * dk
    o_b, ssm_new = _gdn(proj3, small3, wts["w_conv_gdn"], wts["a_log"], wts["dt_bias"], wts["w_norm_gdn"],
                        conv_state8, s0, n_heads=hb, dk=dk, chunk=chunk,
                        n_sub=4 if (t_valid == chunk and t % (4 * chunk) == 0) else 1, t_valid=t_valid,
                        qkv_block=(3 * wa_width) // (3 * wv), z_block=(3 * wa_width + 3 * wv) // wv)
    gate_a_col = 3 * wa_width + 4 * wv
    merged = _merge(o_a.reshape(m, wa_width), o_b.reshape(m, wv), proj, wts["w_branch_a"], wts["w_branch_b"],
                    gate_a_col=gate_a_col, gate_b_col=gate_a_col + d, tm=tm, tn=1024)
    h = _matmul_res(merged, wts["w_out"], x2, tm=tm, tn=1024)
    n_keep = FFN_CONV - 1
    if ffn_state8 is None:
        tm_seq = _row_tile(t, tm)
        hidden, tails = _ffn_in(h, wts["w_norm_ffn"], wts["w_ffn_in"], wts["w_ffn_conv"], wts["b_ffn_conv"],
                                seq=t, d_ff=d_ff, tm=tm_seq, tn=512)
        ffn_new = tails.reshape(b, t // tm_seq, ROW_TILE, d_ff)[:, -1, ROW_TILE - n_keep:]
    else:
        gu = _norm_matmul(h, wts["w_norm_ffn"], wts["w_ffn_in"], tm=tm, tn=1024).reshape(b, t, 2 * d_ff)
        hidden = _ffn_act(gu, wts["w_ffn_conv"], wts["b_ffn_conv"], ffn_state8, d_ff=d_ff, tt=t, tf=d_ff)
        ffn_new = gu[:, t_valid - n_keep:t_valid, :d_ff]
    y = _down_norm(hidden.reshape(m, d_ff), wts["w_ffn_down"], h, wts["w_norm_final"], tm=tm, tk=512)
    return y.reshape(b, t, d), proj3, kv_heads, ffn_new, ssm_new


def kernel(x_prompt, x_sample, cache_k, cache_v, state_conv, state_ssm, state_ffn_conv, page_table, w_norm_mix, w_in, lambda_q1, lambda_k1, lambda_q2, lambda_k2, w_subln, w_conv_gdn, a_log, dt_bias, w_norm_gdn, w_branch_a, w_branch_b, w_out, w_norm_ffn, w_ffn_in, w_ffn_conv, b_ffn_conv, w_ffn_down, w_norm_final):
    depth = w_in.shape[0]
    assert depth == 1, "single-layer trunk"
    l = 0
    bp, seq, d = x_prompt.shape
    bs, dec_seq, _ = x_sample.shape
    ha = cache_k.shape[3]
    width_a = ha * cache_k.shape[4]
    hb, dk = state_ssm.shape[2], state_ssm.shape[3]
    wv = hb * dk
    d_ff = w_ffn_conv.shape[2]
    n_pool, page = cache_k.shape[1], cache_k.shape[2]
    dims = dict(hb=hb, dk=dk, d_ff=d_ff, width_a=width_a)
    lam_init = 0.8 - 0.6 * math.exp(-0.3 * l)
    slopes = 2.0 ** (-8.0 * jnp.arange(1, ha + 1, dtype=F32) / ha)

    c_small = 3 * width_a + 4 * wv
    w_in_l = w_in[l]
    wts = dict(
        w_norm_mix=w_norm_mix[l],
        w_in_main=jnp.concatenate([w_in_l[:, :c_small], w_in_l[:, c_small + 2 * hb:]], axis=1).astype(BF16),
        w_in_small=jnp.pad(w_in_l[:, c_small:c_small + 2 * hb], ((0, 0), (0, LANE - 2 * hb))).astype(BF16),
        w_conv_gdn=w_conv_gdn[l], a_log=a_log[l], dt_bias=dt_bias[l], w_norm_gdn=w_norm_gdn[l],
        w_branch_a=w_branch_a[l].astype(BF16), w_branch_b=w_branch_b[l].astype(BF16),
        w_out=w_out[l].astype(BF16), w_norm_ffn=w_norm_ffn[l], w_ffn_in=w_ffn_in[l].astype(BF16),
        w_ffn_conv=w_ffn_conv[l], b_ffn_conv=b_ffn_conv[l], w_ffn_down=w_ffn_down[l].astype(BF16),
        w_norm_final=w_norm_final,
    )
    lam_params = jnp.zeros((ROW_TILE, LANE), F32)
    lam_params = (lam_params.at[0, :HEAD_DIM_A].set(lambda_q1[l]).at[1, :HEAD_DIM_A].set(lambda_k1[l])
                  .at[2, :HEAD_DIM_A].set(lambda_q2[l]).at[3, :HEAD_DIM_A].set(lambda_k2[l]))

    attend_p = functools.partial(_attn_prompt, slopes=slopes, lam_params=lam_params, w_subln=w_subln[l],
                                 n_heads=ha, lam_init=lam_init, tq=512)
    trunk_p = _trunk(x_prompt, attend_p, (None, None), None, wts, dims=dims, chunk=GDN_CHUNK, t_valid=GDN_CHUNK)

    assert dec_seq >= GDN_CONV - 1 and dec_seq >= FFN_CONV - 1
    pad_t = SAMPLE_ROWS - dec_seq
    x_s = jnp.pad(x_sample, ((0, 0), (0, pad_t), (0, 0)))
    conv_state8 = jnp.pad(state_conv[l], ((0, 0), (ROW_TILE - (GDN_CONV - 1), 0), (0, 0)))
    ffn_state8 = jnp.pad(state_ffn_conv[l], ((0, 0), (ROW_TILE - (FFN_CONV - 1), 0), (0, 0)))
    ck = cache_k.reshape(depth * n_pool, page, ha, width_a // ha)
    cv = cache_v.reshape(depth * n_pool, page, ha, width_a // ha)
    attend_s = functools.partial(_attn_sample, cache_k=ck, cache_v=cv, page_table=page_table + l * n_pool,
                                 slopes=slopes, lam_params=lam_params, w_subln=w_subln[l], n_heads=ha,
                                 dec_seq=dec_seq, lam_init=lam_init, n_group=16)
    trunk_s = _trunk(x_s, attend_s, (conv_state8, state_ssm[l]), ffn_state8, wts, dims=dims,
                     chunk=SAMPLE_ROWS, t_valid=dec_seq)

    def outputs(trunk_out, t_real):
        y, proj, (k_heads, v_heads), ffn_new, ssm = trunk_out
        conv_new = proj[:, t_real - (GDN_CONV - 1):t_real, 3 * width_a:3 * width_a + 3 * wv][None]
        return y[:, :t_real], k_heads[None, :, :t_real], v_heads[None, :, :t_real], conv_new, ssm[None], ffn_new[None]

    yp, kp, vp, cp, sp, fp = outputs(trunk_p, seq)
    ys, ks, vs, cs, ss, fs = outputs(trunk_s, dec_seq)
    return (yp, ys, kp, vp, cp, sp, fp, ks, vs, cs, ss, fs)
```

```python
import functools
import math

import jax
import jax.numpy as jnp
from jax import lax
from jax.experimental import pallas as pl
from jax.experimental.pallas import tpu as pltpu

F32 = jnp.float32
BF16 = jnp.bfloat16
EPS = 1e-6
NEG = -1e30
LOG2E = math.log2(math.e)
ROW_TILE = 8
SAMPLE_ROWS = 16
LANE = 128
VMEM_LIMIT = 60 * 1024 * 1024

HEAD_DIM_A = 64
GDN_CONV = 4
GDN_CHUNK = 64
FFN_CONV = 3


def _cparams(*sem):
    return pltpu.CompilerParams(dimension_semantics=sem, vmem_limit_bytes=VMEM_LIMIT)


def _dot(a, b):
    return jnp.dot(a, b, preferred_element_type=F32)


def _dot_nt(a, b):
    return lax.dot_general(a, b, (((1,), (1,)), ((), ())), preferred_element_type=F32)


def _dot_tn(a, b):
    return lax.dot_general(a, b, (((0,), (0,)), ((), ())), preferred_element_type=F32)


def _sigmoid(x):
    return 0.5 * jnp.tanh(0.5 * x) + 0.5


def _silu(x):
    return x * _sigmoid(x)


def _norm_matmul_kernel(*refs, has_small, norm_rows, head_tiles):
    if has_small:
        x_ref, g_ref, w_ref, w2_ref, o_ref, o2_ref, *head_refs, xn_ref = refs
    else:
        x_ref, g_ref, w_ref, o_ref, xn_ref = refs

    @pl.when(pl.program_id(1) == 0)
    def _():
        def body(r, carry):
            rows = pl.ds(pl.multiple_of(r * norm_rows, norm_rows), norm_rows)
            x = x_ref[rows, :]
            xn = x * lax.rsqrt(jnp.mean(x * x, axis=-1, keepdims=True) + EPS) * g_ref[...]
            xn_ref[rows, :] = xn.astype(BF16)
            return carry
        lax.fori_loop(0, x_ref.shape[0] // norm_rows, body, 0)
        if has_small:
            o2_ref[...] = _dot(xn_ref[...], w2_ref[...])

    res = _dot(xn_ref[...], w_ref[...])
    o_ref[...] = res
    for tile, ref in zip(head_tiles, head_refs if has_small else ()):
        @pl.when(pl.program_id(1) == tile)
        def _(ref=ref):
            ref[...] = res.reshape(ref.shape)


def _norm_matmul(x, g, w, w2=None, *, tm, tn, head_tiles=()):
    m, k = x.shape
    n = w.shape[1]
    has_small = w2 is not None
    in_specs = [pl.BlockSpec((tm, k), lambda i, j: (i, 0)),
                pl.BlockSpec((1, k), lambda i, j: (0, 0)),
                pl.BlockSpec((k, tn), lambda i, j: (0, j))]
    out_specs = [pl.BlockSpec((tm, tn), lambda i, j: (i, j))]
    out_shape = [jax.ShapeDtypeStruct((m, n), F32)]
    args = [x, g.reshape(1, k), w]
    if has_small:
        in_specs.append(pl.BlockSpec((k, LANE), lambda i, j: (0, 0)))
        out_specs.append(pl.BlockSpec((tm, LANE), lambda i, j: (i, 0)))
        out_shape.append(jax.ShapeDtypeStruct((m, LANE), F32))
        args.append(w2)
        for _ in head_tiles:
            out_specs.append(pl.BlockSpec((tm, tn // LANE, LANE), lambda i, j: (i, 0, 0)))
            out_shape.append(jax.ShapeDtypeStruct((m, tn // LANE, LANE), F32))
    outs = pl.pallas_call(
        functools.partial(_norm_matmul_kernel, has_small=has_small, norm_rows=min(tm, 64),
                          head_tiles=tuple(head_tiles)),
        grid=(m // tm, n // tn),
        in_specs=in_specs, out_specs=out_specs, out_shape=out_shape,
        scratch_shapes=[pltpu.VMEM((tm, k), BF16)],
        compiler_params=_cparams("parallel", "arbitrary"),
        name="norm_matmul",
    )(*args)
    return outs if has_small else outs[0]


def _lambda_value(lam_ref, lam_init):
    p = lam_ref[...]
    t1 = jnp.sum(p[0:1, :] * p[1:2, :], axis=-1, keepdims=True)
    t2 = jnp.sum(p[2:3, :] * p[3:4, :], axis=-1, keepdims=True)
    return jnp.exp(t1) - jnp.exp(t2) + lam_init


def _subln(o, w, lam_init):
    n = o * lax.rsqrt(jnp.mean(o * o, axis=-1, keepdims=True) + EPS) * w
    return n * (1.0 - lam_init)


def _attn_prompt_kernel(slope_ref, q_ref, k_ref, v_ref, rel_ref, lam_ref, wsub_ref, o_ref,
                        kb_sc, vt_sc, bias_sc, q2_sc, *, tq, n_q, lam_init):
    slope2 = slope_ref[pl.program_id(1)] * LOG2E
    kb_sc[...] = k_ref[...].astype(BF16)
    vt_sc[...] = v_ref[...].T.astype(BF16)
    bias_sc[...] = slope2 * rel_ref[...]

    lane = lax.broadcasted_iota(jnp.int32, (1, LANE), 1)
    for blk in range(n_q):
        q = q_ref[blk * tq:(blk + 1) * tq, :] * (HEAD_DIM_A ** -0.5 * LOG2E)
        q2_sc[blk, 0:tq, :] = jnp.where(lane < HEAD_DIM_A, q, 0.0).astype(BF16)
        q2_sc[blk, tq:2 * tq, :] = jnp.where(lane >= HEAD_DIM_A, q, 0.0).astype(BF16)

    def scores(ki, blk):
        return _dot_nt(kb_sc[ki * tq:(ki + 1) * tq, :], q2_sc[blk])

    def update(state, s, ki, n_before):
        m_prev, l_prev, acc = state
        bias = bias_sc[...]
        if ki == n_before:
            bias = jnp.where(rel_ref[...] <= 0.0, bias, NEG)
        s = s + bias
        shift = slope2 * float((ki - n_before) * tq)
        m_new = jnp.maximum(m_prev, jnp.max(s, axis=0, keepdims=True) + shift)
        alpha = jnp.exp2(m_prev - m_new)
        p = jnp.exp2(s - (m_new - shift))
        l_new = alpha * l_prev + jnp.sum(p, axis=0, keepdims=True)
        acc = alpha * acc + _dot(vt_sc[:, ki * tq:(ki + 1) * tq], p.astype(BF16))
        return m_new, l_new, acc

    lam = _lambda_value(lam_ref, lam_init)
    for n_before in range(n_q):
        state = (jnp.full((1, 2 * tq), NEG, F32), jnp.zeros((1, 2 * tq), F32),
                 jnp.zeros((LANE, 2 * tq), F32))
        s_cur = scores(0, n_before)
        for ki in range(n_before):
            s_next = scores(ki + 1, n_before)
            state = update(state, s_cur, ki, n_before)
            s_cur = s_next
        _, l_fin, acc = update(state, s_cur, n_before, n_before)
        n = acc / l_fin
        o = (n[:, 0:tq] - lam * n[:, tq:2 * tq]).T
        o_ref[n_before * tq:(n_before + 1) * tq, :] = _subln(o, wsub_ref[...], lam_init).astype(o_ref.dtype)


def _attn_prompt(proj, slopes, lam_params, w_subln, *, n_heads, lam_init, tq):
    b, t, _ = proj.shape
    rel = (jnp.arange(tq, dtype=jnp.int32)[:, None] - jnp.arange(tq, dtype=jnp.int32)[None, :]).astype(F32)
    rel = jnp.concatenate([rel, rel], axis=1)
    hh = n_heads
    grid_spec = pltpu.PrefetchScalarGridSpec(
        num_scalar_prefetch=0,
        grid=(b, hh),
        in_specs=[
            pl.BlockSpec(memory_space=pltpu.SMEM),
            pl.BlockSpec((None, t, LANE), lambda bi, h: (bi, 0, h)),
            pl.BlockSpec((None, t, LANE), lambda bi, h: (bi, 0, hh + h)),
            pl.BlockSpec((None, t, LANE), lambda bi, h: (bi, 0, 2 * hh + h)),
            pl.BlockSpec((tq, 2 * tq), lambda bi, h: (0, 0)),
            pl.BlockSpec((ROW_TILE, LANE), lambda bi, h: (0, 0)),
            pl.BlockSpec((1, LANE), lambda bi, h: (0, 0)),
        ],
        out_specs=pl.BlockSpec((None, t, LANE), lambda bi, h: (bi, 0, h)),
        scratch_shapes=[pltpu.VMEM((t, LANE), BF16), pltpu.VMEM((LANE, t), BF16),
                        pltpu.VMEM((tq, 2 * tq), F32), pltpu.VMEM((t // tq, 2 * tq, LANE), BF16)],
    )
    return pl.pallas_call(
        functools.partial(_attn_prompt_kernel, tq=tq, n_q=t // tq, lam_init=lam_init),
        grid_spec=grid_spec,
        out_shape=jax.ShapeDtypeStruct((b, t, n_heads * LANE), BF16),
        compiler_params=_cparams("parallel", "parallel"),
        name="attn_prompt",
    )(slopes, proj, proj, proj, rel, lam_params, w_subln.reshape(1, LANE))


def _attn_sample_kernel(pt_ref, q_ref, kn_ref, vn_ref, slope_ref, trow_ref, lam_ref, wsub_ref, *rest,
                        n_group, n_heads, dec_seq, past_len, page, lam_init):
    k_refs = rest[:n_group]
    v_refs = rest[n_group:2 * n_group]
    o_ref, qm_sc, bias_sc, m_sc, l_sc, acc_sc = rest[2 * n_group:]
    p_id = pl.program_id(1)
    hs = n_heads * ROW_TILE
    hshift = int(math.log2(n_heads))
    slope2 = slope_ref[:, 0:1] * LOG2E
    trow = trow_ref[:, 0:1]

    def col_info(n_cols):
        row = lax.broadcasted_iota(jnp.int32, (hs, n_cols), 0)
        col = lax.broadcasted_iota(jnp.int32, (hs, n_cols), 1)
        same_head = (col & (n_heads - 1)) == (row >> int(math.log2(ROW_TILE)))
        return same_head, (col >> hshift).astype(F32)

    @pl.when(p_id == 0)
    def _():
        q = q_ref[...] * (HEAD_DIM_A ** -0.5 * LOG2E)
        row = lax.broadcasted_iota(jnp.int32, (hs, LANE), 0)
        lane = lax.broadcasted_iota(jnp.int32, (hs, LANE), 1)
        second_map = ((row >> int(math.log2(dec_seq))) & 1) == 1
        qm = jnp.where((lane >= HEAD_DIM_A) == second_map, q, 0.0).astype(BF16)
        qm_sc[...] = qm
        same_head, key = col_info(page * n_heads)
        bias_sc[...] = jnp.where(same_head, slope2 * key, NEG)
        same_head, key = col_info(ROW_TILE * n_heads)
        dist = trow - key
        s = _dot_nt(qm, kn_ref[...].astype(BF16)) - slope2 * dist
        s = jnp.where(same_head & (dist >= 0.0) & (key < dec_seq), s, NEG)
        m0 = jnp.max(s, axis=-1, keepdims=True)
        p = jnp.exp2(s - m0)
        m_sc[...] = m0
        l_sc[...] = jnp.sum(p, axis=-1, keepdims=True)
        acc_sc[...] = _dot(p.astype(BF16), vn_ref[...].astype(BF16))

    qm = qm_sc[...]
    bias = bias_sc[...]
    scores, shifts = [], []
    for g in range(n_group):
        kg = k_refs[g][...].reshape(page * n_heads, LANE).astype(BF16)
        scores.append(_dot_nt(qm, kg) + bias)
        kpos0 = ((p_id * n_group + g) * page).astype(F32)
        shifts.append(slope2 * ((past_len + trow) - kpos0))
    m_prev = m_sc[...]
    m_new = m_prev
    for s, sh in zip(scores, shifts):
        m_new = jnp.maximum(m_new, jnp.max(s, axis=-1, keepdims=True) - sh)
    alpha = jnp.exp2(m_prev - m_new)
    l_new = alpha * l_sc[...]
    acc = alpha * acc_sc[...]
    for g in range(n_group):
        p = jnp.exp2(scores[g] - (m_new + shifts[g]))
        l_new = l_new + jnp.sum(p, axis=-1, keepdims=True)
        vg = v_refs[g][...].reshape(page * n_heads, LANE).astype(BF16)
        acc = acc + _dot(p.astype(BF16), vg)
    m_sc[...] = m_new
    l_sc[...] = l_new
    acc_sc[...] = acc

    @pl.when(p_id == pl.num_programs(1) - 1)
    def _():
        lam = _lambda_value(lam_ref, lam_init)
        n = acc_sc[...] / l_sc[...]
        o = n - lam * pltpu.roll(n, hs - dec_seq, axis=0)
        o = _subln(o, wsub_ref[...], lam_init)
        pad = jnp.zeros((o_ref.shape[0] - ROW_TILE, LANE), F32)
        for hd in range(n_heads):
            rows = jnp.concatenate([o[hd * ROW_TILE:(hd + 1) * ROW_TILE], pad], axis=0)
            o_ref[:, hd * LANE:(hd + 1) * LANE] = rows.astype(o_ref.dtype)


def _attn_sample(proj, cache_k, cache_v, page_table, slopes, lam_params, w_subln, *,
                 n_heads, dec_seq, lam_init, n_group):
    assert 2 * dec_seq == ROW_TILE
    b, t_pad, _ = proj.shape
    n_pages = page_table.shape[1]
    page = cache_k.shape[1]
    width = n_heads * LANE
    hs = n_heads * ROW_TILE
    past_len = n_pages * page
    q4 = proj[:, :dec_seq, :width].reshape(b, dec_seq, n_heads, LANE).transpose(0, 2, 1, 3)
    q_rows = jnp.concatenate([q4, q4], axis=2).reshape(b, hs, LANE)
    kn_rows = proj[:, :ROW_TILE, width:2 * width].reshape(b, ROW_TILE * n_heads, LANE)
    vn_rows = proj[:, :ROW_TILE, 2 * width:3 * width].reshape(b, ROW_TILE * n_heads, LANE)
    rows = jnp.arange(hs)
    slope_rows = jnp.broadcast_to(slopes[rows // ROW_TILE][:, None], (hs, LANE)).astype(F32)
    trow = jnp.broadcast_to((rows % dec_seq).astype(F32)[:, None], (hs, LANE))
    const = lambda bi, p, pt: (0, 0)

    def page_spec(g):
        return pl.BlockSpec((None, page, n_heads, LANE), lambda bi, p, pt: (pt[bi, p * n_group + g], 0, 0, 0))

    grid_spec = pltpu.PrefetchScalarGridSpec(
        num_scalar_prefetch=1,
        grid=(b, n_pages // n_group),
        in_specs=[
            pl.BlockSpec((None, hs, LANE), lambda bi, p, pt: (bi, 0, 0)),
            pl.BlockSpec((None, ROW_TILE * n_heads, LANE), lambda bi, p, pt: (bi, 0, 0)),
            pl.BlockSpec((None, ROW_TILE * n_heads, LANE), lambda bi, p, pt: (bi, 0, 0)),
            pl.BlockSpec((hs, LANE), const),
            pl.BlockSpec((hs, LANE), const),
            pl.BlockSpec((ROW_TILE, LANE), const),
            pl.BlockSpec((1, LANE), const),
        ] + [page_spec(g) for g in range(n_group)] + [page_spec(g) for g in range(n_group)],
        out_specs=pl.BlockSpec((None, t_pad, width), lambda bi, p, pt: (bi, 0, 0)),
        scratch_shapes=[pltpu.VMEM((hs, LANE), BF16), pltpu.VMEM((hs, page * n_heads), F32),
                        pltpu.VMEM((hs, 1), F32), pltpu.VMEM((hs, 1), F32), pltpu.VMEM((hs, LANE), F32)],
    )
    return pl.pallas_call(
        functools.partial(_attn_sample_kernel, n_group=n_group, n_heads=n_heads, dec_seq=dec_seq,
                          past_len=float(past_len), page=page, lam_init=lam_init),
        grid_spec=grid_spec,
        out_shape=jax.ShapeDtypeStruct((b, t_pad, width), BF16),
        compiler_params=_cparams("parallel", "arbitrary"),
        name="attn_sample",
    )(page_table, q_rows, kn_rows, vn_rows, slope_rows, trow, lam_params, w_subln.reshape(1, LANE),
      *([cache_k] * n_group), *([cache_v] * n_group))


def _split3_dot(a_bf16, x):
    x1 = x.astype(BF16)
    r1 = x - x1.astype(F32)
    x2 = r1.astype(BF16)
    x3 = (r1 - x2.astype(F32)).astype(BF16)
    return _dot(a_bf16, x1) + _dot(a_bf16, x2) + _dot(a_bf16, x3)


def _gdn_kernel(*refs, chunk, t_valid, n_heads, dk, has_state):
    if has_state:
        (x_ref, z_ref, ba_ref, wconv_ref, prm_ref, wnorm_ref, cstate_ref, s0_ref,
         o_ref, sout_ref, xbuf, s_sc) = refs
    else:
        (x_ref, z_ref, ba_ref, wconv_ref, prm_ref, wnorm_ref,
         o_ref, sout_ref, xbuf, s_sc) = refs
    n = pl.program_id(1)
    c = chunk
    wqk = n_heads * dk

    @pl.when(n == 0)
    def _():
        if has_state:
            xbuf[0:ROW_TILE, :] = cstate_ref[...]
            s_sc[...] = s0_ref[...]
        else:
            xbuf[0:ROW_TILE, :] = jnp.zeros((ROW_TILE, xbuf.shape[1]), F32)
            s_sc[...] = jnp.zeros(s_sc.shape, F32)

    rows = x_ref.shape[0]
    n_sub = rows // c
    xbuf[ROW_TILE:ROW_TILE + rows, :] = x_ref[...]
    xfull = xbuf[...]
    y = xfull[ROW_TILE:] * wconv_ref[GDN_CONV - 1:GDN_CONV, :]
    for d in range(1, GDN_CONV):
        y = y + pltpu.roll(xfull, d, axis=0)[ROW_TILE:] * wconv_ref[GDN_CONV - 1 - d:GDN_CONV - d, :]
    if rows > ROW_TILE:
        xbuf[0:ROW_TILE, :] = xfull[rows:rows + ROW_TILE]
    act = _silu(y)

    ba = ba_ref[...]
    beta_all = _sigmoid(ba)
    sp_in = ba + prm_ref[1:2, :]
    softplus = jnp.maximum(sp_in, 0.0) + jnp.log(1.0 + jnp.exp(-jnp.abs(sp_in)))
    g_all = -jnp.exp(prm_ref[0:1, :]) * softplus
    if t_valid < c:
        rv = lax.broadcasted_iota(jnp.int32, (rows, 1), 0) < t_valid
        act = jnp.where(rv, act, 0.0)
        beta_all = jnp.where(rv, beta_all, 0.0)
        g_all = jnp.where(rv, g_all, 0.0)

    ri = lax.broadcasted_iota(jnp.int32, (c, c), 0)
    ci = lax.broadcasted_iota(jnp.int32, (c, c), 1)
    strict = ri > ci
    incl = ri >= ci
    eye = (ri == ci).astype(F32)
    tri = incl.astype(BF16)

    units = [(sub, h) for sub in range(n_sub) for h in range(n_heads)]
    n_double = int(math.log2(c)) - 1
    gc_sub = [_split3_dot(tri, g_all[sub * c:(sub + 1) * c]) for sub in range(n_sub)]
    qs, ks, vs = [], [], []
    for sub, h in units:
        r0 = sub * c
        q = act[r0:r0 + c, h * dk:(h + 1) * dk]
        k = act[r0:r0 + c, wqk + h * dk:wqk + (h + 1) * dk]
        qs.append(q * lax.rsqrt(jnp.sum(q * q, axis=-1, keepdims=True) + EPS) * (dk ** -0.5))
        ks.append(k * lax.rsqrt(jnp.sum(k * k, axis=-1, keepdims=True) + EPS))
        vs.append(act[r0:r0 + c, 2 * wqk + h * dk:2 * wqk + (h + 1) * dk])
    betas = [beta_all[sub * c:(sub + 1) * c, h:h + 1] for sub, h in units]
    gcs = [gc_sub[sub][:, n_heads + h:n_heads + h + 1] for sub, h in units]
    decays = [jnp.exp(jnp.where(incl, gc - jnp.sum(eye * gc, axis=0, keepdims=True), NEG)) for gc in gcs]
    kbs = [k * b for k, b in zip(ks, betas)]
    kbf = [k.astype(BF16) for k in ks]
    kq = [_dot_nt(jnp.concatenate([kb.astype(BF16), q.astype(BF16)], axis=0), kf)
          for kb, q, kf in zip(kbs, qs, kbf)]
    xps = [jnp.where(strict, -(m[:c] * d), 0.0) for m, d in zip(kq, decays)]
    tinvs = [eye + x for x in xps]
    xps = [_dot(x.astype(BF16), x.astype(BF16)) for x in xps]
    for i in range(n_double):
        xpb = [x.astype(BF16) for x in xps]
        tinvs = [t + _dot(xb, t.astype(BF16)) for t, xb in zip(tinvs, xpb)]
        if i < n_double - 1:
            xps = [_dot(xb, xb) for xb in xpb]
    egcs = [jnp.exp(gc) for gc in gcs]
    uw = [_dot(t.astype(BF16), jnp.concatenate([(v * b).astype(BF16), (kb * e).astype(BF16)], axis=1))
          for t, v, b, kb, e in zip(tinvs, vs, betas, kbs, egcs)]
    g_lasts = [gc[c - 1:c, :] for gc in gcs]
    k_decs = [(k * jnp.exp(gl - gc)).astype(BF16) for k, gl, gc in zip(ks, g_lasts, gcs)]
    ws_lhs = [jnp.concatenate([m[:, dk:].astype(BF16), (q * e).astype(BF16)], axis=0)
              for m, q, e in zip(uw, qs, egcs)]
    qk_dec = [(m[c:] * d).astype(BF16) for m, d in zip(kq, decays)]
    states = [s_sc[h] for h in range(n_heads)]
    for sub in range(n_sub):
        ids = [sub * n_heads + h for h in range(n_heads)]
        ws_qs = [_dot(ws_lhs[i], s.astype(BF16)) for i, s in zip(ids, states)]
        v_newb = [(uw[i][:, :dk] - x[:c]).astype(BF16) for i, x in zip(ids, ws_qs)]
        outs = [x[c:] + _dot(qk_dec[i], vn) for i, x, vn in zip(ids, ws_qs, v_newb)]
        states = [s * jnp.exp(g_lasts[i]) + _dot_tn(k_decs[i], vn) for i, s, vn in zip(ids, states, v_newb)]
        for h, o in enumerate(outs):
            on = o * lax.rsqrt(jnp.mean(o * o, axis=-1, keepdims=True) + EPS) * wnorm_ref[...]
            zz = z_ref[sub * c:(sub + 1) * c, h * dk:(h + 1) * dk]
            o_ref[sub * c:(sub + 1) * c, h * dk:(h + 1) * dk] = (on * _silu(zz)).astype(o_ref.dtype)
    for h in range(n_heads):
        s_sc[h] = states[h]

    @pl.when(n == pl.num_programs(1) - 1)
    def _():
        sout_ref[...] = s_sc[...]


def _gdn(proj, small, w_conv, a_log, dt_bias, w_norm, conv_state8, s0, *, n_heads, dk, chunk, n_sub, t_valid,
         qkv_block, z_block):
    b, t, _ = proj.shape
    wv = n_heads * dk
    cch = 3 * wv
    has_state = s0 is not None
    wconv8 = jnp.zeros((ROW_TILE, cch), F32).at[:GDN_CONV].set(w_conv)
    prm = jnp.zeros((ROW_TILE, LANE), F32)
    prm = prm.at[0, n_heads:2 * n_heads].set(a_log).at[1, n_heads:2 * n_heads].set(dt_bias)
    const = lambda bi, n: (0, 0)
    assert n_sub == 1 or t_valid == chunk
    rows = n_sub * chunk
    in_specs = [
        pl.BlockSpec((None, rows, cch), lambda bi, n: (bi, n, qkv_block)),
        pl.BlockSpec((None, rows, wv), lambda bi, n: (bi, n, z_block)),
        pl.BlockSpec((None, rows, LANE), lambda bi, n: (bi, n, 0)),
        pl.BlockSpec((ROW_TILE, cch), const),
        pl.BlockSpec((ROW_TILE, LANE), const),
        pl.BlockSpec((1, dk), const),
    ]
    args = [proj, proj, small, wconv8, prm, w_norm.reshape(1, dk)]
    if has_state:
        in_specs += [pl.BlockSpec((None, ROW_TILE, cch), lambda bi, n: (bi, 0, 0)),
                     pl.BlockSpec((None, n_heads, dk, dk), lambda bi, n: (bi, 0, 0, 0))]
        args += [conv_state8, s0]
    o, s_out = pl.pallas_call(
        functools.partial(_gdn_kernel, chunk=chunk, t_valid=t_valid, n_heads=n_heads, dk=dk,
                          has_state=has_state),
        grid=(b, t // rows),
        in_specs=in_specs,
        out_specs=[pl.BlockSpec((None, rows, wv), lambda bi, n: (bi, n, 0)),
                   pl.BlockSpec((None, n_heads, dk, dk), lambda bi, n: (bi, 0, 0, 0))],
        out_shape=[jax.ShapeDtypeStruct((b, t, wv), BF16),
                   jax.ShapeDtypeStruct((b, n_heads, dk, dk), F32)],
        scratch_shapes=[pltpu.VMEM((ROW_TILE + rows, cch), F32), pltpu.VMEM((n_heads, dk, dk), F32)],
        compiler_params=_cparams("parallel", "arbitrary"),
        name="gdn",
    )(*args)
    return o, s_out


def _merge_out_kernel(oa_ref, ob_ref, ga0_ref, ga1_ref, gb0_ref, gb1_ref, x_ref, wa_ref, wb_ref, wo_ref,
                      o_ref, merged_sc):
    half = ga0_ref.shape[1]
    for c, (ga_ref, gb_ref) in enumerate(((ga0_ref, gb0_ref), (ga1_ref, gb1_ref))):
        cols = slice(c * half, (c + 1) * half)
        pa = _dot(oa_ref[...], wa_ref[:, cols])
        pb = _dot(ob_ref[...], wb_ref[:, cols])
        merged_sc[:, cols] = (_sigmoid(ga_ref[...]) * pa + _sigmoid(gb_ref[...]) * pb).astype(BF16)
    o_ref[...] = x_ref[...] + _dot(merged_sc[...], wo_ref[...])


def _merge_out(o_a, o_b, proj, x, w_a, w_b, w_out, *, gate_a_col, gate_b_col, tm):
    m, ka = o_a.shape
    n = w_a.shape[1]
    half = n // 2
    ga, gb = gate_a_col // half, gate_b_col // half
    whole = lambda i: (0, 0)
    return pl.pallas_call(
        _merge_out_kernel,
        grid=(m // tm,),
        in_specs=[pl.BlockSpec((tm, ka), lambda i: (i, 0)),
                  pl.BlockSpec((tm, o_b.shape[1]), lambda i: (i, 0)),
                  pl.BlockSpec((tm, half), lambda i: (i, ga)),
                  pl.BlockSpec((tm, half), lambda i: (i, ga + 1)),
                  pl.BlockSpec((tm, half), lambda i: (i, gb)),
                  pl.BlockSpec((tm, half), lambda i: (i, gb + 1)),
                  pl.BlockSpec((tm, n), lambda i: (i, 0)),
                  pl.BlockSpec(w_a.shape, whole),
                  pl.BlockSpec(w_b.shape, whole),
                  pl.BlockSpec(w_out.shape, whole)],
        out_specs=pl.BlockSpec((tm, n), lambda i: (i, 0)),
        out_shape=jax.ShapeDtypeStruct((m, n), F32),
        scratch_shapes=[pltpu.VMEM((tm, n), BF16)],
        compiler_params=_cparams("parallel"),
        name="merge_out",
    )(o_a, o_b, proj, proj, proj, proj, x, w_a, w_b, w_out)


def _rmsnorm_rows(x, g):
    return (x * lax.rsqrt(jnp.mean(x * x, axis=-1, keepdims=True) + EPS) * g).astype(BF16)


def _ffn_in_kernel(x_ref, xprev_ref, g_ref, wg_ref, wu_ref, wconv_ref, b_ref, o_ref, tail_ref,
                   xn_sc, xp_sc, buf, *, norm_rows, tiles_per_seq):
    tm = x_ref.shape[0]

    @pl.when(pl.program_id(1) == 0)
    def _():
        def body(r, carry):
            rows = pl.ds(pl.multiple_of(r * norm_rows, norm_rows), norm_rows)
            xn_sc[rows, :] = _rmsnorm_rows(x_ref[rows, :], g_ref[...])
            return carry
        lax.fori_loop(0, tm // norm_rows, body, 0)
        xp_sc[...] = _rmsnorm_rows(xprev_ref[...], g_ref[...])

    gate = _dot(xn_sc[...], wg_ref[...])
    up = _dot(xn_sc[...], wu_ref[...])
    prev = _dot(xp_sc[...], wg_ref[...])[xprev_ref.shape[0] - ROW_TILE:]
    prev = jnp.where(pl.program_id(0) % tiles_per_seq > 0, prev, 0.0)
    tail_ref[...] = gate[tm - ROW_TILE:]
    buf[0:ROW_TILE, :] = prev
    buf[ROW_TILE:ROW_TILE + tm, :] = gate
    full = buf[...]
    y = full[ROW_TILE:] * wconv_ref[FFN_CONV - 1:FFN_CONV, :] + b_ref[...]
    for d in range(1, FFN_CONV):
        y = y + pltpu.roll(full, d, axis=0)[ROW_TILE:] * wconv_ref[FFN_CONV - 1 - d:FFN_CONV - d, :]
    o_ref[...] = (_silu(y) * up).astype(o_ref.dtype)


def _ffn_in(x, g, w, w_conv, b_conv, *, seq, d_ff, tm, tn):
    m, k = x.shape
    nf = d_ff // tn
    prev_rows = 2 * ROW_TILE
    ppt = tm // prev_rows
    wconv8 = jnp.zeros((ROW_TILE, d_ff), F32).at[:FFN_CONV].set(w_conv)
    return pl.pallas_call(
        functools.partial(_ffn_in_kernel, norm_rows=min(tm, 64), tiles_per_seq=seq // tm),
        grid=(m // tm, nf),
        in_specs=[pl.BlockSpec((tm, k), lambda i, j: (i, 0)),
                  pl.BlockSpec((prev_rows, k), lambda i, j: (jnp.maximum(i * ppt - 1, 0), 0)),
                  pl.BlockSpec((1, k), lambda i, j: (0, 0)),
                  pl.BlockSpec((k, tn), lambda i, j: (0, j)),
                  pl.BlockSpec((k, tn), lambda i, j: (0, nf + j)),
                  pl.BlockSpec((ROW_TILE, tn), lambda i, j: (0, j)),
                  pl.BlockSpec((1, tn), lambda i, j: (0, j))],
        out_specs=[pl.BlockSpec((tm, tn), lambda i, j: (i, j)),
                   pl.BlockSpec((None, ROW_TILE, tn), lambda i, j: (i, 0, j))],
        out_shape=[jax.ShapeDtypeStruct((m, d_ff), BF16),
                   jax.ShapeDtypeStruct((m // tm, ROW_TILE, d_ff), F32)],
        scratch_shapes=[pltpu.VMEM((tm, k), BF16), pltpu.VMEM((prev_rows, k), BF16),
                        pltpu.VMEM((ROW_TILE + tm, tn), F32)],
        compiler_params=_cparams("parallel", "arbitrary"),
        name="ffn_in",
    )(x, x, g.reshape(1, k), w, w, wconv8, b_conv.reshape(1, d_ff))


def _ffn_act_kernel(gate_ref, prev_ref, up_ref, wconv_ref, b_ref, o_ref, buf, *, from_state):
    tt = gate_ref.shape[0]
    prev = prev_ref[...]
    if not from_state:
        prev = jnp.where(pl.program_id(1) > 0, prev, 0.0)
    buf[0:ROW_TILE, :] = prev
    buf[ROW_TILE:ROW_TILE + tt, :] = gate_ref[...]
    full = buf[...]
    y = full[ROW_TILE:] * wconv_ref[FFN_CONV - 1:FFN_CONV, :] + b_ref[...]
    for d in range(1, FFN_CONV):
        y = y + pltpu.roll(full, d, axis=0)[ROW_TILE:] * wconv_ref[FFN_CONV - 1 - d:FFN_CONV - d, :]
    o_ref[...] = (_silu(y) * up_ref[...]).astype(o_ref.dtype)


def _ffn_act(gu, w_conv, b_conv, state8, *, d_ff, tt, tf):
    b, t, _ = gu.shape
    from_state = state8 is not None
    nf = d_ff // tf
    wconv8 = jnp.zeros((ROW_TILE, d_ff), F32).at[:FFN_CONV].set(w_conv)
    tpb = tt // ROW_TILE
    if from_state:
        prev_arr = state8
        prev_spec = pl.BlockSpec((None, ROW_TILE, tf), lambda bi, ti, j: (bi, 0, j))
    else:
        prev_arr = gu
        prev_spec = pl.BlockSpec((None, ROW_TILE, tf), lambda bi, ti, j: (bi, jnp.maximum(ti * tpb - 1, 0), j))
    return pl.pallas_call(
        functools.partial(_ffn_act_kernel, from_state=from_state),
        grid=(b, t // tt, nf),
        in_specs=[pl.BlockSpec((None, tt, tf), lambda bi, ti, j: (bi, ti, j)),
                  prev_spec,
                  pl.BlockSpec((None, tt, tf), lambda bi, ti, j: (bi, ti, nf + j)),
                  pl.BlockSpec((ROW_TILE, tf), lambda bi, ti, j: (0, j)),
                  pl.BlockSpec((1, tf), lambda bi, ti, j: (0, j))],
        out_specs=pl.BlockSpec((None, tt, tf), lambda bi, ti, j: (bi, ti, j)),
        out_shape=jax.ShapeDtypeStruct((b, t, d_ff), BF16),
        scratch_shapes=[pltpu.VMEM((ROW_TILE + tt, tf), F32)],
        compiler_params=_cparams("parallel", "parallel", "parallel"),
        name="ffn_act",
    )(gu, prev_arr, gu, wconv8, b_conv.reshape(1, d_ff))


def _down_norm_kernel(a_ref, w_ref, r_ref, g_ref, o_ref, *, norm_rows):
    kk = pl.program_id(1)

    @pl.when(kk == 0)
    def _():
        o_ref[...] = r_ref[...]

    o_ref[...] += _dot(a_ref[...], w_ref[...])

    @pl.when(kk == pl.num_programs(1) - 1)
    def _():
        def body(r, carry):
            rows = pl.ds(pl.multiple_of(r * norm_rows, norm_rows), norm_rows)
            hh = o_ref[rows, :]
            o_ref[rows, :] = hh * lax.rsqrt(jnp.mean(hh * hh, axis=-1, keepdims=True) + EPS) * g_ref[...]
            return carry
        lax.fori_loop(0, o_ref.shape[0] // norm_rows, body, 0)


def _down_norm(a, w, resid, g, *, tm, tk):
    m, k = a.shape
    n = w.shape[1]
    return pl.pallas_call(
        functools.partial(_down_norm_kernel, norm_rows=min(tm, 64)),
        grid=(m // tm, k // tk),
        in_specs=[pl.BlockSpec((tm, tk), lambda i, kk: (i, kk)),
                  pl.BlockSpec((tk, n), lambda i, kk: (kk, 0)),
                  pl.BlockSpec((tm, n), lambda i, kk: (i, 0)),
                  pl.BlockSpec((1, n), lambda i, kk: (0, 0))],
        out_specs=pl.BlockSpec((tm, n), lambda i, kk: (i, 0)),
        out_shape=jax.ShapeDtypeStruct((m, n), F32),
        compiler_params=_cparams("parallel", "arbitrary"),
        name="down_norm",
    )(a, w, resid, g.reshape(1, n))


def _row_tile(rows, cap):
    tile = rows
    while tile > cap:
        assert tile % 2 == 0
        tile //= 2
    return tile


def _trunk(x, attend, gdn_state, ffn_state8, wts, *, dims, chunk, t_valid):
    b, t, d = x.shape
    m = b * t
    hb, dk, d_ff = dims["hb"], dims["dk"], dims["d_ff"]
    wa_width = dims["width_a"]
    x2 = x.reshape(m, d)
    tm = _row_tile(m, 1024)
    tn_in = wa_width
    proj, small, k_heads, v_heads = _norm_matmul(x2, wts["w_norm_mix"], wts["w_in_main"], wts["w_in_small"],
                                                 tm=tm, tn=tn_in, head_tiles=(1, 2))
    kv_heads = tuple(a.reshape(b, t, wa_width // LANE, LANE) for a in (k_heads, v_heads))
    cols = proj.shape[1]
    proj3 = proj.reshape(b, t, cols)
    small3 = small.reshape(b, t, LANE)
    o_a = attend(proj3)
    conv_state8, s0 = gdn_state
    wv = hb * dk
    o_b, ssm_new = _gdn(proj3, small3, wts["w_conv_gdn"], wts["a_log"], wts["dt_bias"], wts["w_norm_gdn"],
                        conv_state8, s0, n_heads=hb, dk=dk, chunk=chunk,
                        n_sub=4 if (t_valid == chunk and t % (4 * chunk) == 0) else 1, t_valid=t_valid,
                        qkv_block=(3 * wa_width) // (3 * wv), z_block=(3 * wa_width + 3 * wv) // wv)
    gate_a_col = 3 * wa_width + 4 * wv
    h = _merge_out(o_a.reshape(m, wa_width), o_b.reshape(m, wv), proj, x2, wts["w_branch_a"], wts["w_branch_b"],
                   wts["w_out"], gate_a_col=gate_a_col, gate_b_col=gate_a_col + d, tm=_row_tile(m, 256))
    n_keep = FFN_CONV - 1
    if ffn_state8 is None:
        tm_seq = _row_tile(t, tm)
        hidden, tails = _ffn_in(h, wts["w_norm_ffn"], wts["w_ffn_in"], wts["w_ffn_conv"], wts["b_ffn_conv"],
                                seq=t, d_ff=d_ff, tm=tm_seq, tn=512)
        ffn_new = tails.reshape(b, t // tm_seq, ROW_TILE, d_ff)[:, -1, ROW_TILE - n_keep:]
    else:
        gu = _norm_matmul(h, wts["w_norm_ffn"], wts["w_ffn_in"], tm=tm, tn=1024).reshape(b, t, 2 * d_ff)
        hidden = _ffn_act(gu, wts["w_ffn_conv"], wts["b_ffn_conv"], ffn_state8, d_ff=d_ff, tt=t, tf=d_ff)
        ffn_new = gu[:, t_valid - n_keep:t_valid, :d_ff]
    y = _down_norm(hidden.reshape(m, d_ff), wts["w_ffn_down"], h, wts["w_norm_final"], tm=tm, tk=512)
    return y.reshape(b, t, d), proj3, kv_heads, ffn_new, ssm_new


def kernel(x_prompt, x_sample, cache_k, cache_v, state_conv, state_ssm, state_ffn_conv, page_table, w_norm_mix, w_in, lambda_q1, lambda_k1, lambda_q2, lambda_k2, w_subln, w_conv_gdn, a_log, dt_bias, w_norm_gdn, w_branch_a, w_branch_b, w_out, w_norm_ffn, w_ffn_in, w_ffn_conv, b_ffn_conv, w_ffn_down, w_norm_final):
    depth = w_in.shape[0]
    assert depth == 1, "single-layer trunk"
    l = 0
    bp, seq, d = x_prompt.shape
    bs, dec_seq, _ = x_sample.shape
    ha = cache_k.shape[3]
    width_a = ha * cache_k.shape[4]
    hb, dk = state_ssm.shape[2], state_ssm.shape[3]
    wv = hb * dk
    d_ff = w_ffn_conv.shape[2]
    n_pool, page = cache_k.shape[1], cache_k.shape[2]
    dims = dict(hb=hb, dk=dk, d_ff=d_ff, width_a=width_a)
    lam_init = 0.8 - 0.6 * math.exp(-0.3 * l)
    slopes = 2.0 ** (-8.0 * jnp.arange(1, ha + 1, dtype=F32) / ha)

    c_small = 3 * width_a + 4 * wv
    w_in_l = w_in[l].astype(BF16)
    wts = dict(
        w_norm_mix=w_norm_mix[l],
        w_in_main=jnp.concatenate([w_in_l[:, :c_small], w_in_l[:, c_small + 2 * hb:]], axis=1),
        w_in_small=jnp.pad(w_in_l[:, c_small:c_small + 2 * hb], ((0, 0), (0, LANE - 2 * hb))),
        w_conv_gdn=w_conv_gdn[l], a_log=a_log[l], dt_bias=dt_bias[l], w_norm_gdn=w_norm_gdn[l],
        w_branch_a=w_branch_a[l].astype(BF16), w_branch_b=w_branch_b[l].astype(BF16),
        w_out=w_out[l].astype(BF16), w_norm_ffn=w_norm_ffn[l], w_ffn_in=w_ffn_in[l].astype(BF16),
        w_ffn_conv=w_ffn_conv[l], b_ffn_conv=b_ffn_conv[l], w_ffn_down=w_ffn_down[l].astype(BF16),
        w_norm_final=w_norm_final,
    )
    lam_params = jnp.zeros((ROW_TILE, LANE), F32)
    lam_params = (lam_params.at[0, :HEAD_DIM_A].set(lambda_q1[l]).at[1, :HEAD_DIM_A].set(lambda_k1[l])
                  .at[2, :HEAD_DIM_A].set(lambda_q2[l]).at[3, :HEAD_DIM_A].set(lambda_k2[l]))

    attend_p = functools.partial(_attn_prompt, slopes=slopes, lam_params=lam_params, w_subln=w_subln[l],
                                 n_heads=ha, lam_init=lam_init, tq=512)
    trunk_p = _trunk(x_prompt, attend_p, (None, None), None, wts, dims=dims, chunk=GDN_CHUNK, t_valid=GDN_CHUNK)

    assert dec_seq >= GDN_CONV - 1 and dec_seq >= FFN_CONV - 1
    pad_t = SAMPLE_ROWS - dec_seq
    x_s = jnp.pad(x_sample, ((0, 0), (0, pad_t), (0, 0)))
    conv_state8 = jnp.pad(state_conv[l], ((0, 0), (ROW_TILE - (GDN_CONV - 1), 0), (0, 0)))
    ffn_state8 = jnp.pad(state_ffn_conv[l], ((0, 0), (ROW_TILE - (FFN_CONV - 1), 0), (0, 0)))
    ck = cache_k.reshape(depth * n_pool, page, ha, width_a // ha)
    cv = cache_v.reshape(depth * n_pool, page, ha, width_a // ha)
    attend_s = functools.partial(_attn_sample, cache_k=ck, cache_v=cv, page_table=page_table + l * n_pool,
                                 slopes=slopes, lam_params=lam_params, w_subln=w_subln[l], n_heads=ha,
                                 dec_seq=dec_seq, lam_init=lam_init, n_group=16)
    trunk_s = _trunk(x_s, attend_s, (conv_state8, state_ssm[l]), ffn_state8, wts, dims=dims,
                     chunk=SAMPLE_ROWS, t_valid=dec_seq)

    def outputs(trunk_out, t_real):
        y, proj, (k_heads, v_heads), ffn_new, ssm = trunk_out
        conv_new = proj[:, t_real - (GDN_CONV - 1):t_real, 3 * width_a:3 * width_a + 3 * wv][None]
        return y[:, :t_real], k_heads[None, :, :t_real], v_heads[None, :, :t_real], conv_new, ssm[None], ffn_new[None]

    yp, kp, vp, cp, sp, fp = outputs(trunk_p, seq)
    ys, ks, vs, cs, ss, fs = outputs(trunk_s, dec_seq)
    return (yp, ys, kp, vp, cp, sp, fp, ks, vs, cs, ss, fs)
```

```python
import functools
import math

import jax
import jax.numpy as jnp
from jax import lax
from jax.experimental import pallas as pl
from jax.experimental.pallas import tpu as pltpu

F32 = jnp.float32
BF16 = jnp.bfloat16
EPS = 1e-6
NEG = -1e30
LOG2E = math.log2(math.e)
ROW_TILE = 8
SAMPLE_ROWS = 16
LANE = 128
VMEM_LIMIT = 60 * 1024 * 1024

HEAD_DIM_A = 64
GDN_CONV = 4
GDN_CHUNK = 64
FFN_CONV = 3


def _cparams(*sem):
    return pltpu.CompilerParams(dimension_semantics=sem, vmem_limit_bytes=VMEM_LIMIT)


def _dot(a, b):
    return jnp.dot(a, b, preferred_element_type=F32)


def _dot_nt(a, b):
    return lax.dot_general(a, b, (((1,), (1,)), ((), ())), preferred_element_type=F32)


def _dot_tn(a, b):
    return lax.dot_general(a, b, (((0,), (0,)), ((), ())), preferred_element_type=F32)


def _sigmoid(x):
    return 0.5 * jnp.tanh(0.5 * x) + 0.5


def _silu(x):
    return x * _sigmoid(x)


def _norm_matmul_kernel(*refs, has_small, norm_rows, head_tiles):
    if has_small:
        x_ref, g_ref, w_ref, w2_ref, o_ref, o2_ref, *head_refs, xn_ref = refs
    else:
        x_ref, g_ref, w_ref, o_ref, xn_ref = refs

    @pl.when(pl.program_id(1) == 0)
    def _():
        def body(r, carry):
            rows = pl.ds(pl.multiple_of(r * norm_rows, norm_rows), norm_rows)
            x = x_ref[rows, :]
            xn = x * lax.rsqrt(jnp.mean(x * x, axis=-1, keepdims=True) + EPS) * g_ref[...]
            xn_ref[rows, :] = xn.astype(BF16)
            return carry
        lax.fori_loop(0, x_ref.shape[0] // norm_rows, body, 0)
        if has_small:
            o2_ref[...] = _dot(xn_ref[...], w2_ref[...])

    res = _dot(xn_ref[...], w_ref[...])
    o_ref[...] = res
    for tile, ref in zip(head_tiles, head_refs if has_small else ()):
        @pl.when(pl.program_id(1) == tile)
        def _(ref=ref):
            ref[...] = res.reshape(ref.shape)


def _norm_matmul(x, g, w, w2=None, *, tm, tn, head_tiles=()):
    m, k = x.shape
    n = w.shape[1]
    has_small = w2 is not None
    in_specs = [pl.BlockSpec((tm, k), lambda i, j: (i, 0)),
                pl.BlockSpec((1, k), lambda i, j: (0, 0)),
                pl.BlockSpec((k, tn), lambda i, j: (0, j))]
    out_specs = [pl.BlockSpec((tm, tn), lambda i, j: (i, j))]
    out_shape = [jax.ShapeDtypeStruct((m, n), F32)]
    args = [x, g.reshape(1, k), w]
    if has_small:
        in_specs.append(pl.BlockSpec((k, LANE), lambda i, j: (0, 0)))
        out_specs.append(pl.BlockSpec((tm, LANE), lambda i, j: (i, 0)))
        out_shape.append(jax.ShapeDtypeStruct((m, LANE), F32))
        args.append(w2)
        for _ in head_tiles:
            out_specs.append(pl.BlockSpec((tm, tn // LANE, LANE), lambda i, j: (i, 0, 0)))
            out_shape.append(jax.ShapeDtypeStruct((m, tn // LANE, LANE), F32))
    outs = pl.pallas_call(
        functools.partial(_norm_matmul_kernel, has_small=has_small, norm_rows=min(tm, 64),
                          head_tiles=tuple(head_tiles)),
        grid=(m // tm, n // tn),
        in_specs=in_specs, out_specs=out_specs, out_shape=out_shape,
        scratch_shapes=[pltpu.VMEM((tm, k), BF16)],
        compiler_params=_cparams("parallel", "arbitrary"),
        name="norm_matmul",
    )(*args)
    return outs if has_small else outs[0]


def _lambda_value(lam_ref, lam_init):
    p = lam_ref[...]
    t1 = jnp.sum(p[0:1, :] * p[1:2, :], axis=-1, keepdims=True)
    t2 = jnp.sum(p[2:3, :] * p[3:4, :], axis=-1, keepdims=True)
    return jnp.exp(t1) - jnp.exp(t2) + lam_init


def _subln(o, w, lam_init):
    n = o * lax.rsqrt(jnp.mean(o * o, axis=-1, keepdims=True) + EPS) * w
    return n * (1.0 - lam_init)


def _attn_prompt_kernel(slope_ref, q_ref, k_ref, v_ref, rel_ref, lam_ref, wsub_ref, o_ref,
                        kb_sc, vt_sc, bias_sc, q2_sc, *, tq, n_q, lam_init):
    slope2 = slope_ref[pl.program_id(1)] * LOG2E
    kb_sc[...] = k_ref[...].astype(BF16)
    vt_sc[...] = v_ref[...].T.astype(BF16)
    rel = rel_ref[...]
    bias_sc[0] = slope2 * rel
    bias_sc[1] = jnp.where(rel <= 0.0, slope2 * rel, NEG)

    lane = lax.broadcasted_iota(jnp.int32, (1, LANE), 1)
    for blk in range(n_q):
        q = q_ref[blk * tq:(blk + 1) * tq, :] * (HEAD_DIM_A ** -0.5 * LOG2E)
        q2_sc[blk, 0:tq, :] = jnp.where(lane < HEAD_DIM_A, q, 0.0).astype(BF16)
        q2_sc[blk, tq:2 * tq, :] = jnp.where(lane >= HEAD_DIM_A, q, 0.0).astype(BF16)

    def scores(ki, blk):
        return _dot_nt(kb_sc[ki * tq:(ki + 1) * tq, :], q2_sc[blk])

    def update(state, s, ki, n_before):
        m_prev, l_prev, acc = state
        s = s + bias_sc[1 if ki == n_before else 0]
        shift = slope2 * float((ki - n_before) * tq)
        m_new = jnp.maximum(m_prev, jnp.max(s, axis=0, keepdims=True) + shift)
        alpha = jnp.exp2(m_prev - m_new)
        p = jnp.exp2(s - (m_new - shift))
        l_new = alpha * l_prev + jnp.sum(p, axis=0, keepdims=True)
        acc = alpha * acc + _dot(vt_sc[:, ki * tq:(ki + 1) * tq], p.astype(BF16))
        return m_new, l_new, acc

    lam = _lambda_value(lam_ref, lam_init)
    for n_before in range(n_q):
        state = (jnp.full((1, 2 * tq), NEG, F32), jnp.zeros((1, 2 * tq), F32),
                 jnp.zeros((LANE, 2 * tq), F32))
        s_cur = scores(0, n_before)
        for ki in range(n_before):
            s_next = scores(ki + 1, n_before)
            state = update(state, s_cur, ki, n_before)
            s_cur = s_next
        _, l_fin, acc = update(state, s_cur, n_before, n_before)
        n = acc / l_fin
        o = (n[:, 0:tq] - lam * n[:, tq:2 * tq]).T
        o_ref[n_before * tq:(n_before + 1) * tq, :] = _subln(o, wsub_ref[...], lam_init).astype(o_ref.dtype)


def _attn_prompt(proj, slopes, lam_params, w_subln, *, n_heads, lam_init, tq):
    b, t, _ = proj.shape
    rel = (jnp.arange(tq, dtype=jnp.int32)[:, None] - jnp.arange(tq, dtype=jnp.int32)[None, :]).astype(F32)
    rel = jnp.concatenate([rel, rel], axis=1)
    hh = n_heads
    grid_spec = pltpu.PrefetchScalarGridSpec(
        num_scalar_prefetch=0,
        grid=(b, hh),
        in_specs=[
            pl.BlockSpec(memory_space=pltpu.SMEM),
            pl.BlockSpec((None, t, LANE), lambda bi, h: (bi, 0, h)),
            pl.BlockSpec((None, t, LANE), lambda bi, h: (bi, 0, hh + h)),
            pl.BlockSpec((None, t, LANE), lambda bi, h: (bi, 0, 2 * hh + h)),
            pl.BlockSpec((tq, 2 * tq), lambda bi, h: (0, 0)),
            pl.BlockSpec((ROW_TILE, LANE), lambda bi, h: (0, 0)),
            pl.BlockSpec((1, LANE), lambda bi, h: (0, 0)),
        ],
        out_specs=pl.BlockSpec((None, t, LANE), lambda bi, h: (bi, 0, h)),
        scratch_shapes=[pltpu.VMEM((t, LANE), BF16), pltpu.VMEM((LANE, t), BF16),
                        pltpu.VMEM((2, tq, 2 * tq), F32), pltpu.VMEM((t // tq, 2 * tq, LANE), BF16)],
    )
    return pl.pallas_call(
        functools.partial(_attn_prompt_kernel, tq=tq, n_q=t // tq, lam_init=lam_init),
        grid_spec=grid_spec,
        out_shape=jax.ShapeDtypeStruct((b, t, n_heads * LANE), BF16),
        compiler_params=_cparams("parallel", "parallel"),
        name="attn_prompt",
    )(slopes, proj, proj, proj, rel, lam_params, w_subln.reshape(1, LANE))


def _attn_sample_kernel(pt_ref, q_ref, kn_ref, vn_ref, slope_ref, trow_ref, lam_ref, wsub_ref, *rest,
                        n_group, n_heads, dec_seq, past_len, page, lam_init):
    k_refs = rest[:n_group]
    v_refs = rest[n_group:2 * n_group]
    o_ref, qm_sc, bias_sc, m_sc, l_sc, acc_sc = rest[2 * n_group:]
    p_id = pl.program_id(1)
    hs = n_heads * ROW_TILE
    hshift = int(math.log2(n_heads))
    slope2 = slope_ref[:, 0:1] * LOG2E
    trow = trow_ref[:, 0:1]

    def col_info(n_cols):
        row = lax.broadcasted_iota(jnp.int32, (hs, n_cols), 0)
        col = lax.broadcasted_iota(jnp.int32, (hs, n_cols), 1)
        same_head = (col & (n_heads - 1)) == (row >> int(math.log2(ROW_TILE)))
        return same_head, (col >> hshift).astype(F32)

    @pl.when(p_id == 0)
    def _():
        q = q_ref[...] * (HEAD_DIM_A ** -0.5 * LOG2E)
        row = lax.broadcasted_iota(jnp.int32, (hs, LANE), 0)
        lane = lax.broadcasted_iota(jnp.int32, (hs, LANE), 1)
        second_map = ((row >> int(math.log2(dec_seq))) & 1) == 1
        qm = jnp.where((lane >= HEAD_DIM_A) == second_map, q, 0.0).astype(BF16)
        qm_sc[...] = qm
        same_head, key = col_info(page * n_heads)
        bias_sc[...] = jnp.where(same_head, slope2 * key, NEG)
        same_head, key = col_info(ROW_TILE * n_heads)
        dist = trow - key
        s = _dot_nt(qm, kn_ref[...].astype(BF16)) - slope2 * dist
        s = jnp.where(same_head & (dist >= 0.0) & (key < dec_seq), s, NEG)
        m0 = jnp.max(s, axis=-1, keepdims=True)
        p = jnp.exp2(s - m0)
        m_sc[...] = m0
        l_sc[...] = jnp.sum(p, axis=-1, keepdims=True)
        acc_sc[...] = _dot(p.astype(BF16), vn_ref[...].astype(BF16))

    qm = qm_sc[...]
    bias = bias_sc[...]
    scores, shifts = [], []
    for g in range(n_group):
        kg = k_refs[g][...].reshape(page * n_heads, LANE).astype(BF16)
        scores.append(_dot_nt(qm, kg) + bias)
        kpos0 = ((p_id * n_group + g) * page).astype(F32)
        shifts.append(slope2 * ((past_len + trow) - kpos0))
    m_prev = m_sc[...]
    m_new = m_prev
    for s, sh in zip(scores, shifts):
        m_new = jnp.maximum(m_new, jnp.max(s, axis=-1, keepdims=True) - sh)
    alpha = jnp.exp2(m_prev - m_new)
    l_new = alpha * l_sc[...]
    acc = alpha * acc_sc[...]
    for g in range(n_group):
        p = jnp.exp2(scores[g] - (m_new + shifts[g]))
        l_new = l_new + jnp.sum(p, axis=-1, keepdims=True)
        vg = v_refs[g][...].reshape(page * n_heads, LANE).astype(BF16)
        acc = acc + _dot(p.astype(BF16), vg)
    m_sc[...] = m_new
    l_sc[...] = l_new
    acc_sc[...] = acc

    @pl.when(p_id == pl.num_programs(1) - 1)
    def _():
        lam = _lambda_value(lam_ref, lam_init)
        n = acc_sc[...] / l_sc[...]
        o = n - lam * pltpu.roll(n, hs - dec_seq, axis=0)
        o = _subln(o, wsub_ref[...], lam_init)
        pad = jnp.zeros((o_ref.shape[0] - ROW_TILE, LANE), F32)
        for hd in range(n_heads):
            rows = jnp.concatenate([o[hd * ROW_TILE:(hd + 1) * ROW_TILE], pad], axis=0)
            o_ref[:, hd * LANE:(hd + 1) * LANE] = rows.astype(o_ref.dtype)


def _attn_sample(proj, cache_k, cache_v, page_table, slopes, lam_params, w_subln, *,
                 n_heads, dec_seq, lam_init, n_group):
    assert 2 * dec_seq == ROW_TILE
    b, t_pad, _ = proj.shape
    n_pages = page_table.shape[1]
    page = cache_k.shape[1]
    width = n_heads * LANE
    hs = n_heads * ROW_TILE
    past_len = n_pages * page
    q4 = proj[:, :dec_seq, :width].reshape(b, dec_seq, n_heads, LANE).transpose(0, 2, 1, 3)
    q_rows = jnp.concatenate([q4, q4], axis=2).reshape(b, hs, LANE)
    kn_rows = proj[:, :ROW_TILE, width:2 * width].reshape(b, ROW_TILE * n_heads, LANE)
    vn_rows = proj[:, :ROW_TILE, 2 * width:3 * width].reshape(b, ROW_TILE * n_heads, LANE)
    rows = jnp.arange(hs)
    slope_rows = jnp.broadcast_to(slopes[rows // ROW_TILE][:, None], (hs, LANE)).astype(F32)
    trow = jnp.broadcast_to((rows % dec_seq).astype(F32)[:, None], (hs, LANE))
    const = lambda bi, p, pt: (0, 0)

    def page_spec(g):
        return pl.BlockSpec((None, page, n_heads, LANE), lambda bi, p, pt: (pt[bi, p * n_group + g], 0, 0, 0))

    grid_spec = pltpu.PrefetchScalarGridSpec(
        num_scalar_prefetch=1,
        grid=(b, n_pages // n_group),
        in_specs=[
            pl.BlockSpec((None, hs, LANE), lambda bi, p, pt: (bi, 0, 0)),
            pl.BlockSpec((None, ROW_TILE * n_heads, LANE), lambda bi, p, pt: (bi, 0, 0)),
            pl.BlockSpec((None, ROW_TILE * n_heads, LANE), lambda bi, p, pt: (bi, 0, 0)),
            pl.BlockSpec((hs, LANE), const),
            pl.BlockSpec((hs, LANE), const),
            pl.BlockSpec((ROW_TILE, LANE), const),
            pl.BlockSpec((1, LANE), const),
        ] + [page_spec(g) for g in range(n_group)] + [page_spec(g) for g in range(n_group)],
        out_specs=pl.BlockSpec((None, t_pad, width), lambda bi, p, pt: (bi, 0, 0)),
        scratch_shapes=[pltpu.VMEM((hs, LANE), BF16), pltpu.VMEM((hs, page * n_heads), F32),
                        pltpu.VMEM((hs, 1), F32), pltpu.VMEM((hs, 1), F32), pltpu.VMEM((hs, LANE), F32)],
    )
    return pl.pallas_call(
        functools.partial(_attn_sample_kernel, n_group=n_group, n_heads=n_heads, dec_seq=dec_seq,
                          past_len=float(past_len), page=page, lam_init=lam_init),
        grid_spec=grid_spec,
        out_shape=jax.ShapeDtypeStruct((b, t_pad, width), BF16),
        compiler_params=_cparams("parallel", "arbitrary"),
        name="attn_sample",
    )(page_table, q_rows, kn_rows, vn_rows, slope_rows, trow, lam_params, w_subln.reshape(1, LANE),
      *([cache_k] * n_group), *([cache_v] * n_group))


def _split3_dot(a_bf16, x):
    x1 = x.astype(BF16)
    r1 = x - x1.astype(F32)
    x2 = r1.astype(BF16)
    x3 = (r1 - x2.astype(F32)).astype(BF16)
    return _dot(a_bf16, x1) + _dot(a_bf16, x2) + _dot(a_bf16, x3)


def _gdn_kernel(*refs, chunk, t_valid, n_heads, dk, has_state):
    if has_state:
        (x_ref, z_ref, ba_ref, wconv_ref, prm_ref, wnorm_ref, cstate_ref, s0_ref,
         o_ref, sout_ref, xbuf, s_sc) = refs
    else:
        (x_ref, z_ref, ba_ref, wconv_ref, prm_ref, wnorm_ref,
         o_ref, sout_ref, xbuf, s_sc) = refs
    n = pl.program_id(1)
    c = chunk
    wqk = n_heads * dk

    @pl.when(n == 0)
    def _():
        if has_state:
            xbuf[0:ROW_TILE, :] = cstate_ref[...]
            s_sc[...] = s0_ref[...]
        else:
            xbuf[0:ROW_TILE, :] = jnp.zeros((ROW_TILE, xbuf.shape[1]), F32)
            s_sc[...] = jnp.zeros(s_sc.shape, F32)

    rows = x_ref.shape[0]
    n_sub = rows // c
    xbuf[ROW_TILE:ROW_TILE + rows, :] = x_ref[...]
    xfull = xbuf[...]
    y = xfull[ROW_TILE:] * wconv_ref[GDN_CONV - 1:GDN_CONV, :]
    for d in range(1, GDN_CONV):
        y = y + pltpu.roll(xfull, d, axis=0)[ROW_TILE:] * wconv_ref[GDN_CONV - 1 - d:GDN_CONV - d, :]
    if rows > ROW_TILE:
        xbuf[0:ROW_TILE, :] = xfull[rows:rows + ROW_TILE]
    act = _silu(y)

    ba = ba_ref[...]
    beta_all = _sigmoid(ba)
    sp_in = ba + prm_ref[1:2, :]
    softplus = jnp.maximum(sp_in, 0.0) + jnp.log(1.0 + jnp.exp(-jnp.abs(sp_in)))
    g_all = -jnp.exp(prm_ref[0:1, :]) * softplus
    if t_valid < c:
        rv = lax.broadcasted_iota(jnp.int32, (rows, 1), 0) < t_valid
        act = jnp.where(rv, act, 0.0)
        beta_all = jnp.where(rv, beta_all, 0.0)
        g_all = jnp.where(rv, g_all, 0.0)

    ri = lax.broadcasted_iota(jnp.int32, (c, c), 0)
    ci = lax.broadcasted_iota(jnp.int32, (c, c), 1)
    strict = ri > ci
    incl = ri >= ci
    eye = (ri == ci).astype(F32)
    tri = incl.astype(BF16)

    units = [(sub, h) for sub in range(n_sub) for h in range(n_heads)]
    n_double = int(math.log2(c)) - 1
    gc_sub = [_split3_dot(tri, g_all[sub * c:(sub + 1) * c]) for sub in range(n_sub)]
    qs, ks, vs = [], [], []
    for sub, h in units:
        r0 = sub * c
        q = act[r0:r0 + c, h * dk:(h + 1) * dk]
        k = act[r0:r0 + c, wqk + h * dk:wqk + (h + 1) * dk]
        qs.append(q * lax.rsqrt(jnp.sum(q * q, axis=-1, keepdims=True) + EPS) * (dk ** -0.5))
        ks.append(k * lax.rsqrt(jnp.sum(k * k, axis=-1, keepdims=True) + EPS))
        vs.append(act[r0:r0 + c, 2 * wqk + h * dk:2 * wqk + (h + 1) * dk])
    betas = [beta_all[sub * c:(sub + 1) * c, h:h + 1] for sub, h in units]
    gcs = [gc_sub[sub][:, n_heads + h:n_heads + h + 1] for sub, h in units]
    decays = [jnp.exp(jnp.where(incl, gc - jnp.sum(eye * gc, axis=0, keepdims=True), NEG)) for gc in gcs]
    kbs = [k * b for k, b in zip(ks, betas)]
    kbf = [k.astype(BF16) for k in ks]
    kq = [_dot_nt(jnp.concatenate([kb.astype(BF16), q.astype(BF16)], axis=0), kf)
          for kb, q, kf in zip(kbs, qs, kbf)]
    xps = [jnp.where(strict, -(m[:c] * d), 0.0) for m, d in zip(kq, decays)]
    tinvs = [eye + x for x in xps]
    xps = [_dot(x.astype(BF16), x.astype(BF16)) for x in xps]
    for i in range(n_double):
        xpb = [x.astype(BF16) for x in xps]
        tinvs = [t + _dot(xb, t.astype(BF16)) for t, xb in zip(tinvs, xpb)]
        if i < n_double - 1:
            xps = [_dot(xb, xb) for xb in xpb]
    egcs = [jnp.exp(gc) for gc in gcs]
    uw = [_dot(t.astype(BF16), jnp.concatenate([(v * b).astype(BF16), (kb * e).astype(BF16)], axis=1))
          for t, v, b, kb, e in zip(tinvs, vs, betas, kbs, egcs)]
    g_lasts = [gc[c - 1:c, :] for gc in gcs]
    k_decs = [(k * jnp.exp(gl - gc)).astype(BF16) for k, gl, gc in zip(ks, g_lasts, gcs)]
    ws_lhs = [jnp.concatenate([m[:, dk:].astype(BF16), (q * e).astype(BF16)], axis=0)
              for m, q, e in zip(uw, qs, egcs)]
    qk_dec = [(m[c:] * d).astype(BF16) for m, d in zip(kq, decays)]
    states = [s_sc[h] for h in range(n_heads)]
    for sub in range(n_sub):
        ids = [sub * n_heads + h for h in range(n_heads)]
        ws_qs = [_dot(ws_lhs[i], s.astype(BF16)) for i, s in zip(ids, states)]
        v_newb = [(uw[i][:, :dk] - x[:c]).astype(BF16) for i, x in zip(ids, ws_qs)]
        outs = [x[c:] + _dot(qk_dec[i], vn) for i, x, vn in zip(ids, ws_qs, v_newb)]
        states = [s * jnp.exp(g_lasts[i]) + _dot_tn(k_decs[i], vn) for i, s, vn in zip(ids, states, v_newb)]
        for h, o in enumerate(outs):
            on = o * lax.rsqrt(jnp.mean(o * o, axis=-1, keepdims=True) + EPS) * wnorm_ref[...]
            zz = z_ref[sub * c:(sub + 1) * c, h * dk:(h + 1) * dk]
            o_ref[sub * c:(sub + 1) * c, h * dk:(h + 1) * dk] = (on * _silu(zz)).astype(o_ref.dtype)
    for h in range(n_heads):
        s_sc[h] = states[h]

    @pl.when(n == pl.num_programs(1) - 1)
    def _():
        sout_ref[...] = s_sc[...]


def _gdn(proj, small, w_conv, a_log, dt_bias, w_norm, conv_state8, s0, *, n_heads, dk, chunk, n_sub, t_valid,
         qkv_block, z_block):
    b, t, _ = proj.shape
    wv = n_heads * dk
    cch = 3 * wv
    has_state = s0 is not None
    wconv8 = jnp.zeros((ROW_TILE, cch), F32).at[:GDN_CONV].set(w_conv)
    prm = jnp.zeros((ROW_TILE, LANE), F32)
    prm = prm.at[0, n_heads:2 * n_heads].set(a_log).at[1, n_heads:2 * n_heads].set(dt_bias)
    const = lambda bi, n: (0, 0)
    assert n_sub == 1 or t_valid == chunk
    rows = n_sub * chunk
    in_specs = [
        pl.BlockSpec((None, rows, cch), lambda bi, n: (bi, n, qkv_block)),
        pl.BlockSpec((None, rows, wv), lambda bi, n: (bi, n, z_block)),
        pl.BlockSpec((None, rows, LANE), lambda bi, n: (bi, n, 0)),
        pl.BlockSpec((ROW_TILE, cch), const),
        pl.BlockSpec((ROW_TILE, LANE), const),
        pl.BlockSpec((1, dk), const),
    ]
    args = [proj, proj, small, wconv8, prm, w_norm.reshape(1, dk)]
    if has_state:
        in_specs += [pl.BlockSpec((None, ROW_TILE, cch), lambda bi, n: (bi, 0, 0)),
                     pl.BlockSpec((None, n_heads, dk, dk), lambda bi, n: (bi, 0, 0, 0))]
        args += [conv_state8, s0]
    o, s_out = pl.pallas_call(
        functools.partial(_gdn_kernel, chunk=chunk, t_valid=t_valid, n_heads=n_heads, dk=dk,
                          has_state=has_state),
        grid=(b, t // rows),
        in_specs=in_specs,
        out_specs=[pl.BlockSpec((None, rows, wv), lambda bi, n: (bi, n, 0)),
                   pl.BlockSpec((None, n_heads, dk, dk), lambda bi, n: (bi, 0, 0, 0))],
        out_shape=[jax.ShapeDtypeStruct((b, t, wv), BF16),
                   jax.ShapeDtypeStruct((b, n_heads, dk, dk), F32)],
        scratch_shapes=[pltpu.VMEM((ROW_TILE + rows, cch), F32), pltpu.VMEM((n_heads, dk, dk), F32)],
        compiler_params=_cparams("parallel", "arbitrary"),
        name="gdn",
    )(*args)
    return o, s_out


def _merge_out_kernel(oa_ref, ob_ref, ga0_ref, ga1_ref, gb0_ref, gb1_ref, x_ref, wa_ref, wb_ref, wo_ref,
                      o_ref, merged_sc):
    half = ga0_ref.shape[1]
    for c, (ga_ref, gb_ref) in enumerate(((ga0_ref, gb0_ref), (ga1_ref, gb1_ref))):
        cols = slice(c * half, (c + 1) * half)
        pa = _dot(oa_ref[...], wa_ref[:, cols])
        pb = _dot(ob_ref[...], wb_ref[:, cols])
        merged_sc[:, cols] = (_sigmoid(ga_ref[...]) * pa + _sigmoid(gb_ref[...]) * pb).astype(BF16)
    o_ref[...] = x_ref[...] + _dot(merged_sc[...], wo_ref[...])


def _merge_out(o_a, o_b, proj, x, w_a, w_b, w_out, *, gate_a_col, gate_b_col, tm):
    m, ka = o_a.shape
    n = w_a.shape[1]
    half = n // 2
    ga, gb = gate_a_col // half, gate_b_col // half
    whole = lambda i: (0, 0)
    return pl.pallas_call(
        _merge_out_kernel,
        grid=(m // tm,),
        in_specs=[pl.BlockSpec((tm, ka), lambda i: (i, 0)),
                  pl.BlockSpec((tm, o_b.shape[1]), lambda i: (i, 0)),
                  pl.BlockSpec((tm, half), lambda i: (i, ga)),
                  pl.BlockSpec((tm, half), lambda i: (i, ga + 1)),
                  pl.BlockSpec((tm, half), lambda i: (i, gb)),
                  pl.BlockSpec((tm, half), lambda i: (i, gb + 1)),
                  pl.BlockSpec((tm, n), lambda i: (i, 0)),
                  pl.BlockSpec(w_a.shape, whole),
                  pl.BlockSpec(w_b.shape, whole),
                  pl.BlockSpec(w_out.shape, whole)],
        out_specs=pl.BlockSpec((tm, n), lambda i: (i, 0)),
        out_shape=jax.ShapeDtypeStruct((m, n), F32),
        scratch_shapes=[pltpu.VMEM((tm, n), BF16)],
        compiler_params=_cparams("parallel"),
        name="merge_out",
    )(o_a, o_b, proj, proj, proj, proj, x, w_a, w_b, w_out)


def _rmsnorm_rows(x, g):
    return (x * lax.rsqrt(jnp.mean(x * x, axis=-1, keepdims=True) + EPS) * g).astype(BF16)


def _ffn_in_kernel(x_ref, xprev_ref, g_ref, wg_ref, wu_ref, wconv_ref, b_ref, o_ref, tail_ref,
                   xn_sc, *, norm_rows, tiles_per_seq):
    tm = x_ref.shape[0]
    prev = xprev_ref.shape[0]

    @pl.when(pl.program_id(1) == 0)
    def _():
        def body(r, carry):
            xn_sc[pl.ds(pl.multiple_of(prev + r * norm_rows, prev), norm_rows), :] = _rmsnorm_rows(
                x_ref[pl.ds(pl.multiple_of(r * norm_rows, norm_rows), norm_rows), :], g_ref[...])
            return carry
        lax.fori_loop(0, tm // norm_rows, body, 0)
        xp = _rmsnorm_rows(xprev_ref[...], g_ref[...])
        xn_sc[0:prev, :] = jnp.where(pl.program_id(0) % tiles_per_seq > 0, xp, jnp.zeros_like(xp))

    gate = _dot(xn_sc[...], wg_ref[...])
    up = _dot(xn_sc[prev:, :], wu_ref[...])
    tail_ref[...] = gate[prev + tm - ROW_TILE:]
    full = gate[prev - ROW_TILE:]
    y = full[ROW_TILE:] * wconv_ref[FFN_CONV - 1:FFN_CONV, :] + b_ref[...]
    for d in range(1, FFN_CONV):
        y = y + pltpu.roll(full, d, axis=0)[ROW_TILE:] * wconv_ref[FFN_CONV - 1 - d:FFN_CONV - d, :]
    o_ref[...] = (_silu(y) * up).astype(o_ref.dtype)


def _ffn_in(x, g, w, w_conv, b_conv, *, seq, d_ff, tm, tn):
    m, k = x.shape
    nf = d_ff // tn
    prev_rows = 2 * ROW_TILE
    ppt = tm // prev_rows
    wconv8 = jnp.zeros((ROW_TILE, d_ff), F32).at[:FFN_CONV].set(w_conv)
    return pl.pallas_call(
        functools.partial(_ffn_in_kernel, norm_rows=min(tm, 64), tiles_per_seq=seq // tm),
        grid=(m // tm, nf),
        in_specs=[pl.BlockSpec((tm, k), lambda i, j: (i, 0)),
                  pl.BlockSpec((prev_rows, k), lambda i, j: (jnp.maximum(i * ppt - 1, 0), 0)),
                  pl.BlockSpec((1, k), lambda i, j: (0, 0)),
                  pl.BlockSpec((k, tn), lambda i, j: (0, j)),
                  pl.BlockSpec((k, tn), lambda i, j: (0, nf + j)),
                  pl.BlockSpec((ROW_TILE, tn), lambda i, j: (0, j)),
                  pl.BlockSpec((1, tn), lambda i, j: (0, j))],
        out_specs=[pl.BlockSpec((tm, tn), lambda i, j: (i, j)),
                   pl.BlockSpec((None, ROW_TILE, tn), lambda i, j: (i, 0, j))],
        out_shape=[jax.ShapeDtypeStruct((m, d_ff), BF16),
                   jax.ShapeDtypeStruct((m // tm, ROW_TILE, d_ff), F32)],
        scratch_shapes=[pltpu.VMEM((prev_rows + tm, k), BF16)],
        compiler_params=_cparams("parallel", "arbitrary"),
        name="ffn_in",
    )(x, x, g.reshape(1, k), w, w, wconv8, b_conv.reshape(1, d_ff))


def _ffn_act_kernel(gate_ref, prev_ref, up_ref, wconv_ref, b_ref, o_ref, buf, *, from_state):
    tt = gate_ref.shape[0]
    prev = prev_ref[...]
    if not from_state:
        prev = jnp.where(pl.program_id(1) > 0, prev, 0.0)
    buf[0:ROW_TILE, :] = prev
    buf[ROW_TILE:ROW_TILE + tt, :] = gate_ref[...]
    full = buf[...]
    y = full[ROW_TILE:] * wconv_ref[FFN_CONV - 1:FFN_CONV, :] + b_ref[...]
    for d in range(1, FFN_CONV):
        y = y + pltpu.roll(full, d, axis=0)[ROW_TILE:] * wconv_ref[FFN_CONV - 1 - d:FFN_CONV - d, :]
    o_ref[...] = (_silu(y) * up_ref[...]).astype(o_ref.dtype)


def _ffn_act(gu, w_conv, b_conv, state8, *, d_ff, tt, tf):
    b, t, _ = gu.shape
    from_state = state8 is not None
    nf = d_ff // tf
    wconv8 = jnp.zeros((ROW_TILE, d_ff), F32).at[:FFN_CONV].set(w_conv)
    tpb = tt // ROW_TILE
    if from_state:
        prev_arr = state8
        prev_spec = pl.BlockSpec((None, ROW_TILE, tf), lambda bi, ti, j: (bi, 0, j))
    else:
        prev_arr = gu
        prev_spec = pl.BlockSpec((None, ROW_TILE, tf), lambda bi, ti, j: (bi, jnp.maximum(ti * tpb - 1, 0), j))
    return pl.pallas_call(
        functools.partial(_ffn_act_kernel, from_state=from_state),
        grid=(b, t // tt, nf),
        in_specs=[pl.BlockSpec((None, tt, tf), lambda bi, ti, j: (bi, ti, j)),
                  prev_spec,
                  pl.BlockSpec((None, tt, tf), lambda bi, ti, j: (bi, ti, nf + j)),
                  pl.BlockSpec((ROW_TILE, tf), lambda bi, ti, j: (0, j)),
                  pl.BlockSpec((1, tf), lambda bi, ti, j: (0, j))],
        out_specs=pl.BlockSpec((None, tt, tf), lambda bi, ti, j: (bi, ti, j)),
        out_shape=jax.ShapeDtypeStruct((b, t, d_ff), BF16),
        scratch_shapes=[pltpu.VMEM((ROW_TILE + tt, tf), F32)],
        compiler_params=_cparams("parallel", "parallel", "parallel"),
        name="ffn_act",
    )(gu, prev_arr, gu, wconv8, b_conv.reshape(1, d_ff))


def _down_norm_kernel(a_ref, w_ref, r_ref, g_ref, o_ref, *, norm_rows):
    kk = pl.program_id(1)

    @pl.when(kk == 0)
    def _():
        o_ref[...] = r_ref[...]

    o_ref[...] += _dot(a_ref[...], w_ref[...])

    @pl.when(kk == pl.num_programs(1) - 1)
    def _():
        def body(r, carry):
            rows = pl.ds(pl.multiple_of(r * norm_rows, norm_rows), norm_rows)
            hh = o_ref[rows, :]
            o_ref[rows, :] = hh * lax.rsqrt(jnp.mean(hh * hh, axis=-1, keepdims=True) + EPS) * g_ref[...]
            return carry
        lax.fori_loop(0, o_ref.shape[0] // norm_rows, body, 0)


def _down_norm(a, w, resid, g, *, tm, tk):
    m, k = a.shape
    n = w.shape[1]
    return pl.pallas_call(
        functools.partial(_down_norm_kernel, norm_rows=min(tm, 64)),
        grid=(m // tm, k // tk),
        in_specs=[pl.BlockSpec((tm, tk), lambda i, kk: (i, kk)),
                  pl.BlockSpec((tk, n), lambda i, kk: (kk, 0)),
                  pl.BlockSpec((tm, n), lambda i, kk: (i, 0)),
                  pl.BlockSpec((1, n), lambda i, kk: (0, 0))],
        out_specs=pl.BlockSpec((tm, n), lambda i, kk: (i, 0)),
        out_shape=jax.ShapeDtypeStruct((m, n), F32),
        compiler_params=_cparams("parallel", "arbitrary"),
        name="down_norm",
    )(a, w, resid, g.reshape(1, n))


def _row_tile(rows, cap):
    tile = rows
    while tile > cap:
        assert tile % 2 == 0
        tile //= 2
    return tile


def _trunk(x, attend, gdn_state, ffn_state8, wts, *, dims, chunk, t_valid):
    b, t, d = x.shape
    m = b * t
    hb, dk, d_ff = dims["hb"], dims["dk"], dims["d_ff"]
    wa_width = dims["width_a"]
    x2 = x.reshape(m, d)
    tm = _row_tile(m, 1024)
    tn_in = wa_width
    proj, small, k_heads, v_heads = _norm_matmul(x2, wts["w_norm_mix"], wts["w_in_main"], wts["w_in_small"],
                                                 tm=tm, tn=tn_in, head_tiles=(1, 2))
    kv_heads = tuple(a.reshape(b, t, wa_width // LANE, LANE) for a in (k_heads, v_heads))
    cols = proj.shape[1]
    proj3 = proj.reshape(b, t, cols)
    small3 = small.reshape(b, t, LANE)
    o_a = attend(proj3)
    conv_state8, s0 = gdn_state
    wv = hb * dk
    o_b, ssm_new = _gdn(proj3, small3, wts["w_conv_gdn"], wts["a_log"], wts["dt_bias"], wts["w_norm_gdn"],
                        conv_state8, s0, n_heads=hb, dk=dk, chunk=chunk,
                        n_sub=4 if (t_valid == chunk and t % (4 * chunk) == 0) else 1, t_valid=t_valid,
                        qkv_block=(3 * wa_width) // (3 * wv), z_block=(3 * wa_width + 3 * wv) // wv)
    gate_a_col = 3 * wa_width + 4 * wv
    h = _merge_out(o_a.reshape(m, wa_width), o_b.reshape(m, wv), proj, x2, wts["w_branch_a"], wts["w_branch_b"],
                   wts["w_out"], gate_a_col=gate_a_col, gate_b_col=gate_a_col + d, tm=_row_tile(m, 256))
    n_keep = FFN_CONV - 1
    if ffn_state8 is None:
        tm_seq = _row_tile(t, tm)
        hidden, tails = _ffn_in(h, wts["w_norm_ffn"], wts["w_ffn_in"], wts["w_ffn_conv"], wts["b_ffn_conv"],
                                seq=t, d_ff=d_ff, tm=tm_seq, tn=512)
        ffn_new = tails.reshape(b, t // tm_seq, ROW_TILE, d_ff)[:, -1, ROW_TILE - n_keep:]
    else:
        gu = _norm_matmul(h, wts["w_norm_ffn"], wts["w_ffn_in"], tm=tm, tn=1024).reshape(b, t, 2 * d_ff)
        hidden = _ffn_act(gu, wts["w_ffn_conv"], wts["b_ffn_conv"], ffn_state8, d_ff=d_ff, tt=t, tf=d_ff)
        ffn_new = gu[:, t_valid - n_keep:t_valid, :d_ff]
    y = _down_norm(hidden.reshape(m, d_ff), wts["w_ffn_down"], h, wts["w_norm_final"], tm=tm, tk=512)
    return y.reshape(b, t, d), proj3, kv_heads, ffn_new, ssm_new


def kernel(x_prompt, x_sample, cache_k, cache_v, state_conv, state_ssm, state_ffn_conv, page_table, w_norm_mix, w_in, lambda_q1, lambda_k1, lambda_q2, lambda_k2, w_subln, w_conv_gdn, a_log, dt_bias, w_norm_gdn, w_branch_a, w_branch_b, w_out, w_norm_ffn, w_ffn_in, w_ffn_conv, b_ffn_conv, w_ffn_down, w_norm_final):
    depth = w_in.shape[0]
    assert depth == 1, "single-layer trunk"
    l = 0
    bp, seq, d = x_prompt.shape
    bs, dec_seq, _ = x_sample.shape
    ha = cache_k.shape[3]
    width_a = ha * cache_k.shape[4]
    hb, dk = state_ssm.shape[2], state_ssm.shape[3]
    wv = hb * dk
    d_ff = w_ffn_conv.shape[2]
    n_pool, page = cache_k.shape[1], cache_k.shape[2]
    dims = dict(hb=hb, dk=dk, d_ff=d_ff, width_a=width_a)
    lam_init = 0.8 - 0.6 * math.exp(-0.3 * l)
    slopes = 2.0 ** (-8.0 * jnp.arange(1, ha + 1, dtype=F32) / ha)

    c_small = 3 * width_a + 4 * wv
    w_in_l = w_in[l].astype(BF16)
    wts = dict(
        w_norm_mix=w_norm_mix[l],
        w_in_main=jnp.concatenate([w_in_l[:, :c_small], w_in_l[:, c_small + 2 * hb:]], axis=1),
        w_in_small=jnp.pad(w_in_l[:, c_small:c_small + 2 * hb], ((0, 0), (0, LANE - 2 * hb))),
        w_conv_gdn=w_conv_gdn[l], a_log=a_log[l], dt_bias=dt_bias[l], w_norm_gdn=w_norm_gdn[l],
        w_branch_a=w_branch_a[l].astype(BF16), w_branch_b=w_branch_b[l].astype(BF16),
        w_out=w_out[l].astype(BF16), w_norm_ffn=w_norm_ffn[l], w_ffn_in=w_ffn_in[l].astype(BF16),
        w_ffn_conv=w_ffn_conv[l], b_ffn_conv=b_ffn_conv[l], w_ffn_down=w_ffn_down[l].astype(BF16),
        w_norm_final=w_norm_final,
    )
    lam_params = jnp.zeros((ROW_TILE, LANE), F32)
    lam_params = (lam_params.at[0, :HEAD_DIM_A].set(lambda_q1[l]).at[1, :HEAD_DIM_A].set(lambda_k1[l])
                  .at[2, :HEAD_DIM_A].set(lambda_q2[l]).at[3, :HEAD_DIM_A].set(lambda_k2[l]))

    attend_p = functools.partial(_attn_prompt, slopes=slopes, lam_params=lam_params, w_subln=w_subln[l],
                                 n_heads=ha, lam_init=lam_init, tq=512)
    trunk_p = _trunk(x_prompt, attend_p, (None, None), None, wts, dims=dims, chunk=GDN_CHUNK, t_valid=GDN_CHUNK)

    assert dec_seq >= GDN_CONV - 1 and dec_seq >= FFN_CONV - 1
    pad_t = SAMPLE_ROWS - dec_seq
    x_s = jnp.pad(x_sample, ((0, 0), (0, pad_t), (0, 0)))
    conv_state8 = jnp.pad(state_conv[l], ((0, 0), (ROW_TILE - (GDN_CONV - 1), 0), (0, 0)))
    ffn_state8 = jnp.pad(state_ffn_conv[l], ((0, 0), (ROW_TILE - (FFN_CONV - 1), 0), (0, 0)))
    ck = cache_k.reshape(depth * n_pool, page, ha, width_a // ha)
    cv = cache_v.reshape(depth * n_pool, page, ha, width_a // ha)
    attend_s = functools.partial(_attn_sample, cache_k=ck, cache_v=cv, page_table=page_table + l * n_pool,
                                 slopes=slopes, lam_params=lam_params, w_subln=w_subln[l], n_heads=ha,
                                 dec_seq=dec_seq, lam_init=lam_init, n_group=16)
    trunk_s = _trunk(x_s, attend_s, (conv_state8, state_ssm[l]), ffn_state8, wts, dims=dims,
                     chunk=SAMPLE_ROWS, t_valid=dec_seq)

    def outputs(trunk_out, t_real):
        y, proj, (k_heads, v_heads), ffn_new, ssm = trunk_out
        conv_new = proj[:, t_real - (GDN_CONV - 1):t_real, 3 * width_a:3 * width_a + 3 * wv][None]
        return y[:, :t_real], k_heads[None, :, :t_real], v_heads[None, :, :t_real], conv_new, ssm[None], ffn_new[None]

    yp, kp, vp, cp, sp, fp = outputs(trunk_p, seq)
    ys, ks, vs, cs, ss, fs = outputs(trunk_s, dec_seq)
    return (yp, ys, kp, vp, cp, sp, fp, ks, vs, cs, ss, fs)
```
